```python
import jax, jax.numpy as jnp
from jax import lax
import numpy as np

D_MODEL = 2048
BATCH = 2
SEQ = 4096
DEPTH = 2

N_MEM = 256
RWKV_HEADS = 16
RWKV_HEAD_DIM = 64
RWKV_WIDTH = RWKV_HEADS * RWKV_HEAD_DIM
DECAY_RANK = 64
ICLR_RANK = 64
GATE_RANK = 128
DECAY_SCALE = 0.606531
GN_EPS = 64e-5
CONV_WIDTH = 1024
CONV_K = 3
XATTN_HEADS = 4
XATTN_HEAD_DIM = 256
XATTN_WIDTH = XATTN_HEADS * XATTN_HEAD_DIM
N_BRANCH = 3
BRANCH_WIDTH = 1024
D_FF = 5504
RWKV_PROJ = 3 * RWKV_WIDTH + 2 * DECAY_RANK + 2 * ICLR_RANK + GATE_RANK
CONV_PROJ = 3 * CONV_WIDTH
GATE_PROJ = N_BRANCH * D_MODEL
D_IN = RWKV_PROJ + CONV_PROJ + XATTN_WIDTH + GATE_PROJ
ALPHA = (2 * DEPTH) ** 0.25
BETA = (8 * DEPTH) ** -0.25
LN_EPS = 1e-5

kernel_name = "hybrid_rwkv7_shortconv_memxattn_macaron_deepnorm"


def layer_norm(x, g, b):
    xf = x.astype(jnp.float32)
    mu = jnp.mean(xf, axis=-1, keepdims=True)
    var = jnp.mean(jnp.square(xf - mu), axis=-1, keepdims=True)
    y = (xf - mu) * lax.rsqrt(var + LN_EPS)
    return (y * g.astype(jnp.float32) + b.astype(jnp.float32)).astype(x.dtype)


def swiglu(x, w_gate, w_up, w_down):
    return (jax.nn.silu(x @ w_gate) * (x @ w_up)) @ w_down


def _wkv7_scan(r, w, k, v, kk, a, reverse):
    b, t, h, n = r.shape

    def step(S, inp):
        r_t, w_t, k_t, v_t, kk_t, a_t = inp
        s_kk = jnp.einsum('bhvk,bhk->bhv', S, kk_t)
        S = (S * w_t[:, :, None, :]
             - s_kk[..., None] * (kk_t * a_t)[:, :, None, :]
             + v_t[..., None] * k_t[:, :, None, :])
        y_t = jnp.einsum('bhvk,bhk->bhv', S, r_t)
        return S, y_t

    xs = tuple(jnp.moveaxis(u, 1, 0) for u in (r, w, k, v, kk, a))
    s0 = jnp.zeros((b, h, n, n), jnp.float32)
    _, ys = lax.scan(step, s0, xs, reverse=reverse)
    return jnp.moveaxis(ys, 0, 1)


def rwkv7_branch(p, mu, w0, w_up, a0, a_up, g_up, k_k, k_a, r_k, gn_g, gn_b):
    b, t, _ = p.shape
    h, n = RWKV_HEADS, RWKV_HEAD_DIM
    prev = jnp.pad(p[:, :-1], ((0, 0), (1, 0), (0, 0)))
    nxt = jnp.pad(p[:, 1:], ((0, 0), (0, 1), (0, 0)))
    p = p + mu * (0.5 * (prev + nxt) - p)
    cuts = [RWKV_WIDTH, 2 * RWKV_WIDTH, 3 * RWKV_WIDTH,
            3 * RWKV_WIDTH + 2 * DECAY_RANK, 3 * RWKV_WIDTH + 2 * DECAY_RANK + 2 * ICLR_RANK]
    r, k, v, hw, ha, hg = jnp.split(p, cuts, axis=-1)
    hw = hw.reshape(b, t, 2, DECAY_RANK)
    ha = ha.reshape(b, t, 2, ICLR_RANK)
    w_logit = w0 + jnp.einsum('btdr,drc->btdc', jnp.tanh(hw), w_up)
    decay = jnp.exp(-DECAY_SCALE * jax.nn.sigmoid(w_logit.astype(jnp.float32)))
    a = jax.nn.sigmoid((a0 + jnp.einsum('btdr,drc->btdc', ha, a_up)).astype(jnp.float32))
    g = jax.nn.sigmoid(hg) @ g_up
    rf = r.astype(jnp.float32)
    kf = k.astype(jnp.float32)
    vf = v.astype(jnp.float32)
    r_h = rf.reshape(b, t, h, n)
    v_h = vf.reshape(b, t, h, n)
    kk = (kf * k_k.astype(jnp.float32)).reshape(b, t, h, n)
    kk = kk * lax.rsqrt(jnp.sum(kk * kk, axis=-1, keepdims=True) + 1e-12)
    k_dir = kf[:, :, None, :] * (1.0 + (a - 1.0) * k_a.astype(jnp.float32))
    k_dir_h = k_dir.reshape(b, t, 2, h, n)
    a_h = a.reshape(b, t, 2, h, n)
    decay_h = decay.reshape(b, t, 2, h, n)
    y_fwd = _wkv7_scan(r_h, decay_h[:, :, 0], k_dir_h[:, :, 0], v_h, kk, a_h[:, :, 0], reverse=False)
    y_bwd = _wkv7_scan(r_h, decay_h[:, :, 1], k_dir_h[:, :, 1], v_h, kk, a_h[:, :, 1], reverse=True)
    y = y_fwd + y_bwd
    m = jnp.mean(y, axis=-1, keepdims=True)
    var = jnp.mean(jnp.square(y - m), axis=-1, keepdims=True)
    y = ((y - m) * lax.rsqrt(var + GN_EPS)).reshape(b, t, RWKV_WIDTH)
    y = y * gn_g.astype(jnp.float32) + gn_b.astype(jnp.float32)
    k_bonus = jnp.mean(k_dir_h, axis=2)
    bonus = jnp.sum(r_h * k_bonus * r_k.astype(jnp.float32), axis=-1, keepdims=True) * v_h
    y = y + bonus.reshape(b, t, RWKV_WIDTH)
    return (y * g.astype(jnp.float32)).astype(p.dtype)


def shortconv_branch(p, conv_w):
    gate_b, gate_c, h = jnp.split(p, 3, axis=-1)
    hc = lax.conv_general_dilated(
        gate_c * h, conv_w[:, None, :].astype(p.dtype),
        window_strides=(1,), padding='SAME',
        dimension_numbers=('NWC', 'WIO', 'NWC'),
        feature_group_count=CONV_WIDTH)
    return gate_b * hc


def memory_xattn_branch(q, mem, ln_g, ln_b, w_kv):
    b, t, _ = q.shape
    mem_n = layer_norm(mem, ln_g, ln_b)
    k, v = jnp.split(mem_n @ w_kv, 2, axis=-1)
    qh = q.reshape(b, t, XATTN_HEADS, XATTN_HEAD_DIM)
    kh = k.reshape(b, -1, XATTN_HEADS, XATTN_HEAD_DIM)
    vh = v.reshape(b, -1, XATTN_HEADS, XATTN_HEAD_DIM)
    s = jnp.einsum('bthd,bmhd->bhtm', qh, kh).astype(jnp.float32) * (XATTN_HEAD_DIM ** -0.5)
    attn = jax.nn.softmax(s, axis=-1).astype(vh.dtype)
    o = jnp.einsum('bhtm,bmhd->bthd', attn, vh)
    return o.reshape(b, t, XATTN_WIDTH)


def setup_inputs(seed: int = 0) -> dict:
    key = jax.random.key(seed)
    ks = iter(jax.random.split(key, 40))
    L, D, F = DEPTH, D_MODEL, D_FF

    def nrm(shape, scale):
        return jax.random.normal(next(ks), shape, jnp.float32) * scale

    def gain(shape):
        return 1.0 + nrm(shape, 0.02)

    return {
        "x": nrm((BATCH, SEQ, D), 1.0),
        "mem": nrm((BATCH, N_MEM, D), 1.0),
        "ffn1_w_gate": nrm((L, D, F), D ** -0.5),
        "ffn1_w_up": nrm((L, D, F), D ** -0.5),
        "ffn1_w_down": nrm((L, F, D), BETA * F ** -0.5),
        "ln1_g": gain((L, D)),
        "ln1_b": nrm((L, D), 0.02),
        "w_in": nrm((L, D, D_IN), D ** -0.5),
        "rwkv_mu": jax.random.uniform(next(ks), (L, RWKV_PROJ), jnp.float32),
        "rwkv_w0": nrm((L, 2, RWKV_WIDTH), 1.5),
        "rwkv_w_up": nrm((L, 2, DECAY_RANK, RWKV_WIDTH), 0.5 * DECAY_RANK ** -0.5),
        "rwkv_a0": nrm((L, 2, RWKV_WIDTH), 0.5),
        "rwkv_a_up": nrm((L, 2, ICLR_RANK, RWKV_WIDTH), 0.5 * ICLR_RANK ** -0.5),
        "rwkv_g_up": nrm((L, GATE_RANK, RWKV_WIDTH), GATE_RANK ** -0.5),
        "rwkv_k_k": 0.85 + nrm((L, RWKV_WIDTH), 0.02),
        "rwkv_k_a": gain((L, RWKV_WIDTH)),
        "rwkv_r_k": nrm((L, RWKV_HEADS, RWKV_HEAD_DIM), 0.1),
        "rwkv_gn_g": gain((L, RWKV_WIDTH)),
        "rwkv_gn_b": nrm((L, RWKV_WIDTH), 0.02),
        "conv_w": nrm((L, CONV_K, CONV_WIDTH), CONV_K ** -0.5),
        "mem_ln_g": gain((L, D)),
        "mem_ln_b": nrm((L, D), 0.02),
        "w_mem_kv": nrm((L, D, 2 * XATTN_WIDTH), D ** -0.5),
        "w_branch": nrm((L, N_BRANCH, BRANCH_WIDTH, D), BETA * BRANCH_WIDTH ** -0.5),
        "gate_b": nrm((L, N_BRANCH, D), 0.1),
        "w_out": nrm((L, D, D), BETA * D ** -0.5),
        "ln2_g": gain((L, D)),
        "ln2_b": nrm((L, D), 0.02),
        "ffn2_w_gate": nrm((L, D, F), D ** -0.5),
        "ffn2_w_up": nrm((L, D, F), D ** -0.5),
        "ffn2_w_down": nrm((L, F, D), BETA * F ** -0.5),
        "ln3_g": gain((L, D)),
        "ln3_b": nrm((L, D), 0.02),
    }


def reference(x, mem, ffn1_w_gate, ffn1_w_up, ffn1_w_down, ln1_g, ln1_b, w_in,
              rwkv_mu, rwkv_w0, rwkv_w_up, rwkv_a0, rwkv_a_up, rwkv_g_up, rwkv_k_k,
              rwkv_k_a, rwkv_r_k, rwkv_gn_g, rwkv_gn_b, conv_w, mem_ln_g, mem_ln_b,
              w_mem_kv, w_branch, gate_b, w_out, ln2_g, ln2_b, ffn2_w_gate, ffn2_w_up,
              ffn2_w_down, ln3_g, ln3_b):
    b, t, d = x.shape
    cuts = [RWKV_PROJ, RWKV_PROJ + CONV_PROJ, RWKV_PROJ + CONV_PROJ + XATTN_WIDTH]
    for l in range(DEPTH):
        x = layer_norm(ALPHA * x + 0.5 * swiglu(x, ffn1_w_gate[l], ffn1_w_up[l], ffn1_w_down[l]),
                       ln1_g[l], ln1_b[l])
        p = x @ w_in[l]
        p_rwkv, p_conv, q_mem, gate_logits = jnp.split(p, cuts, axis=-1)
        y_rwkv = rwkv7_branch(p_rwkv, rwkv_mu[l], rwkv_w0[l], rwkv_w_up[l], rwkv_a0[l],
                              rwkv_a_up[l], rwkv_g_up[l], rwkv_k_k[l], rwkv_k_a[l],
                              rwkv_r_k[l], rwkv_gn_g[l], rwkv_gn_b[l])
        y_conv = shortconv_branch(p_conv, conv_w[l])
        y_mem = memory_xattn_branch(q_mem, mem, mem_ln_g[l], mem_ln_b[l], w_mem_kv[l])
        ys = jnp.stack([y_rwkv, y_conv, y_mem], axis=2)
        proj = jnp.einsum('btnc,ncd->btnd', ys, w_branch[l])
        gates = jax.nn.sigmoid(gate_logits.reshape(b, t, N_BRANCH, d) + gate_b[l])
        mixed = jnp.sum(gates * proj, axis=2) @ w_out[l]
        x = layer_norm(ALPHA * x + mixed, ln2_g[l], ln2_b[l])
        x = layer_norm(ALPHA * x + 0.5 * swiglu(x, ffn2_w_gate[l], ffn2_w_up[l], ffn2_w_down[l]),
                       ln3_g[l], ln3_b[l])
    return x
```

```python
import functools

import jax
import jax.numpy as jnp
from jax import lax
from jax.experimental import pallas as pl
from jax.experimental.pallas import tpu as pltpu

F32 = jnp.float32
BF16 = jnp.bfloat16

LANES = 128
SUBLANES = 8
HEAD = 64
CHUNK = 64
XATTN_HEAD = 256
DECAY_SCALE = 0.606531
GN_EPS = 64e-5
LN_EPS = 1e-5
KK_EPS = 1e-12
VMEM_LIMIT = 56 * 1024 * 1024


def _params(*semantics):
    return pltpu.CompilerParams(dimension_semantics=semantics, vmem_limit_bytes=VMEM_LIMIT)


def _dot(a, b):
    return jnp.dot(a.astype(BF16), b.astype(BF16), preferred_element_type=F32)


def _dot_nt(a, b):
    return lax.dot_general(a.astype(BF16), b.astype(BF16), (((1,), (1,)), ((), ())),
                           preferred_element_type=F32)


def _dot_tn(a, b):
    return lax.dot_general(a.astype(BF16), b.astype(BF16), (((0,), (0,)), ((), ())),
                           preferred_element_type=F32)


def _split(x):
    hi = x.astype(BF16)
    lo = (x - hi.astype(F32)).astype(BF16)
    return hi, lo


def _dot_split(a, b):
    ah, al = _split(a)
    bh, bl = _split(b)
    return (jnp.dot(ah, bh, preferred_element_type=F32)
            + jnp.dot(ah, bl, preferred_element_type=F32)
            + jnp.dot(al, bh, preferred_element_type=F32))


def _seg_sum(x, seg):
    hi, lo = _split(x)
    return (jnp.dot(hi, seg, preferred_element_type=F32)
            + jnp.dot(lo, seg, preferred_element_type=F32))


def _layer_norm(z, g, b):
    mu = jnp.mean(z, axis=-1, keepdims=True)
    zc = z - mu
    var = jnp.mean(zc * zc, axis=-1, keepdims=True)
    return zc * lax.rsqrt(var + LN_EPS) * g + b


def _mm_kernel(x_ref, w_ref, o_ref):
    o_ref[...] = jnp.dot(x_ref[...], w_ref[...], preferred_element_type=F32).astype(o_ref.dtype)


def _matmul(x, w, tm, tn, out_dtype=F32):
    m, k = x.shape
    n = w.shape[1]
    return pl.pallas_call(
        _mm_kernel,
        grid=(n // tn, m // tm),
        in_specs=[pl.BlockSpec((tm, k), lambda j, i: (i, 0)),
                  pl.BlockSpec((k, tn), lambda j, i: (0, j))],
        out_specs=pl.BlockSpec((tm, tn), lambda j, i: (i, j)),
        out_shape=jax.ShapeDtypeStruct((m, n), out_dtype),
        compiler_params=_params("parallel", "parallel"),
        name="matmul",
    )(x, w)


def _ffn_kernel(alpha, x_ref, wg_ref, wu_ref, wd_ref, g_ref, b_ref, o_ref, ob_ref, xb_ref, acc_ref):
    f = pl.program_id(1)

    @pl.when(f == 0)
    def _():
        xb_ref[...] = x_ref[...].astype(BF16)
        acc_ref[...] = jnp.zeros_like(acc_ref)

    xb = xb_ref[...]
    hg = jnp.dot(xb, wg_ref[...], preferred_element_type=F32)
    hu = jnp.dot(xb, wu_ref[...], preferred_element_type=F32)
    h = (hg * jax.nn.sigmoid(hg)) * hu
    acc_ref[...] += jnp.dot(h.astype(BF16), wd_ref[...], preferred_element_type=F32)

    @pl.when(f == pl.num_programs(1) - 1)
    def _():
        y = _layer_norm(alpha * x_ref[...] + 0.5 * acc_ref[...], g_ref[...], b_ref[...])
        o_ref[...] = y
        ob_ref[...] = y.astype(BF16)


def _ffn_block(x, wg, wu, wd, g, b, alpha, tm=512, tf=512):
    m, d = x.shape
    fdim = wg.shape[1]
    return pl.pallas_call(
        functools.partial(_ffn_kernel, alpha),
        grid=(m // tm, fdim // tf),
        in_specs=[pl.BlockSpec((tm, d), lambda i, f: (i, 0)),
                  pl.BlockSpec((d, tf), lambda i, f: (0, f)),
                  pl.BlockSpec((d, tf), lambda i, f: (0, f)),
                  pl.BlockSpec((tf, d), lambda i, f: (f, 0)),
                  pl.BlockSpec((1, d), lambda i, f: (0, 0)),
                  pl.BlockSpec((1, d), lambda i, f: (0, 0))],
        out_specs=[pl.BlockSpec((tm, d), lambda i, f: (i, 0)),
                   pl.BlockSpec((tm, d), lambda i, f: (i, 0))],
        out_shape=[jax.ShapeDtypeStruct((m, d), F32), jax.ShapeDtypeStruct((m, d), BF16)],
        scratch_shapes=[pltpu.VMEM((tm, d), BF16), pltpu.VMEM((tm, d), F32)],
        compiler_params=_params("parallel", "arbitrary"),
        name="ffn_block",
    )(x, wg, wu, wd, g, b)


def _prep_kernel(width, p_ref, pp_ref, pn_ref, mu_ref, w0_ref, wup_ref, a0_ref, aup_ref, gup_ref,
                 kk_ref, ka_ref, rk_ref, seg_ref,
                 r_o, v_o, kk_o, lw0_o, lw1_o, kd0_o, kd1_o, bb0_o, bb1_o, g_o, bonus_o):
    i = pl.program_id(1)
    p = p_ref[...]
    tm = p.shape[0]
    prev_row = jnp.where(i > 0, pp_ref[SUBLANES - 1:SUBLANES, :], 0.0)
    next_row = jnp.where(i < pl.num_programs(1) - 1, pn_ref[0:1, :], 0.0)
    rid = lax.broadcasted_iota(jnp.int32, p.shape, 0)
    p_prev = jnp.where(rid == 0, prev_row, pltpu.roll(p, 1, axis=0))
    p_next = jnp.where(rid == tm - 1, next_row, pltpu.roll(p, tm - 1, axis=0))
    ps = p + mu_ref[...] * (0.5 * (p_prev + p_next) - p)

    w = width
    r = ps[:, 0:w]
    k = ps[:, w:2 * w]
    v = ps[:, 2 * w:3 * w]
    hw = ps[:, 3 * w:3 * w + LANES]
    ha = ps[:, 3 * w + LANES:3 * w + 2 * LANES]
    hg = ps[:, 3 * w + 2 * LANES:3 * w + 3 * LANES]

    lw = -DECAY_SCALE * jax.nn.sigmoid(w0_ref[...] + _dot_split(jnp.tanh(hw), wup_ref[...]))
    a = jax.nn.sigmoid(a0_ref[...] + _dot_split(ha, aup_ref[...]))
    g = _dot_split(jax.nn.sigmoid(hg), gup_ref[...])
    seg = seg_ref[...]
    kkr = k * kk_ref[...]
    kk = kkr * lax.rsqrt(_seg_sum(kkr * kkr, seg) + KK_EPS)
    a0 = a[:, 0:w]
    a1 = a[:, w:2 * w]
    ka = ka_ref[...]
    kd0 = k * (1.0 + (a0 - 1.0) * ka)
    kd1 = k * (1.0 + (a1 - 1.0) * ka)
    bonus = _seg_sum(r * (0.5 * (kd0 + kd1)) * rk_ref[...], seg) * v

    r_o[...] = r
    v_o[...] = v
    kk_o[...] = kk
    lw0_o[...] = lw[:, 0:w]
    lw1_o[...] = lw[:, w:2 * w]
    kd0_o[...] = kd0
    kd1_o[...] = kd1
    bb0_o[...] = kk * a0
    bb1_o[...] = kk * a1
    g_o[...] = g
    bonus_o[...] = bonus


def _rwkv_prep(p, mu, w0, wup, a0, aup, gup, k_k, k_a, r_k, seg, tm=256):
    b, t, pw = p.shape
    w = k_k.shape[1]
    nblk = t // SUBLANES
    per = tm // SUBLANES
    row = lambda n: pl.BlockSpec((1, n), lambda bi, i: (0, 0))
    full = lambda a: pl.BlockSpec(a.shape, lambda bi, i: (0, 0))
    out_spec = pl.BlockSpec((None, tm, w), lambda bi, i: (bi, i, 0))
    return pl.pallas_call(
        functools.partial(_prep_kernel, w),
        grid=(b, t // tm),
        in_specs=[pl.BlockSpec((None, tm, pw), lambda bi, i: (bi, i, 0)),
                  pl.BlockSpec((None, SUBLANES, pw), lambda bi, i: (bi, jnp.maximum(i * per - 1, 0), 0)),
                  pl.BlockSpec((None, SUBLANES, pw), lambda bi, i: (bi, jnp.minimum((i + 1) * per, nblk - 1), 0)),
                  row(pw), row(2 * w), full(wup), row(2 * w), full(aup), full(gup),
                  row(w), row(w), row(w), full(seg)],
        out_specs=[out_spec] * 11,
        out_shape=[jax.ShapeDtypeStruct((b, t, w), F32)] * 11,
        compiler_params=_params("parallel", "parallel"),
        name="rwkv_prep",
    )(p, p, p, mu, w0, wup, a0, aup, gup, k_k, k_a, r_k, seg)


def _wkv_chain(r, v, kk, lw, kd, bb, s_ref, reverse):
    c = CHUNK
    n2 = 2 * c
    lane = lax.broadcasted_iota(jnp.int32, (c, LANES), 1)
    head0 = lane < HEAD

    def stack(x):
        return jnp.concatenate([jnp.where(head0, x, 0.0), jnp.where(head0, 0.0, x)], axis=0)

    ti = lax.broadcasted_iota(jnp.int32, (c, c), 0)
    si = lax.broadcasted_iota(jnp.int32, (c, c), 1)
    cmat = jnp.where((si >= ti) if reverse else (si <= ti), 1.0, 0.0).astype(BF16)
    hi = lw.astype(BF16)
    rem = lw - hi.astype(F32)
    mid = rem.astype(BF16)
    lo = (rem - mid.astype(F32)).astype(BF16)
    cum = (jnp.dot(cmat, hi, preferred_element_type=F32)
           + jnp.dot(cmat, mid, preferred_element_type=F32)
           + jnp.dot(cmat, lo, preferred_element_type=F32))
    tot = cum[0:1, :] if reverse else cum[c - 1:c, :]
    e_neg = jnp.exp(-cum)
    e_tot = jnp.exp(tot - cum)
    kkt = stack(kk * jnp.exp(cum - lw))
    rt = stack(r * jnp.exp(cum))
    v2 = stack(v)
    bh = bb * e_neg
    kh = kd * e_neg
    bp2 = stack(bb * e_tot)
    kp2 = stack(kd * e_tot)

    ii = lax.broadcasted_iota(jnp.int32, (n2, n2), 0)
    jj = lax.broadcasted_iota(jnp.int32, (n2, n2), 1)
    same = (ii < c) == (jj < c)
    strict = same & ((jj > ii) if reverse else (jj < ii))
    incl = same & ((jj >= ii) if reverse else (jj <= ii))

    g = _dot_nt(jnp.concatenate([kkt, rt], axis=0), jnp.concatenate([bh, bh, kh, kh], axis=0))
    a_bd = jnp.where(strict, g[0:n2, 0:n2], 0.0)
    bm_bd = jnp.where(strict, g[0:n2, n2:2 * n2], 0.0)
    mqb_bd = jnp.where(incl, g[n2:2 * n2, 0:n2], 0.0)
    mqk_bd = jnp.where(incl, g[n2:2 * n2, n2:2 * n2], 0.0)

    eye = jnp.where(ii == jj, 1.0, 0.0)
    x = eye - a_bd
    ap = a_bd
    for _ in range((c - 1).bit_length() - 1):
        ap = _dot(ap, ap)
        x = x + _dot(x, ap)

    bv = _dot(bm_bd, v2)
    w12 = _dot(x, jnp.concatenate([kkt, bv], axis=1))
    qw = _dot(mqb_bd, w12)
    g1 = rt - qw[:, 0:LANES]
    g2 = _dot(mqk_bd, v2) - qw[:, LANES:2 * LANES]
    wb = _dot_tn(w12, bp2)
    m_bd = eye * jnp.exp(tot) - wb[0:LANES]
    n_bd = _dot_tn(v2, kp2) - wb[LANES:2 * LANES]

    s0 = s_ref[...]
    y2 = _dot_nt(g1, s0) + g2
    s_ref[...] = _dot(s0, m_bd) + n_bd
    return y2[0:c] + y2[c:n2]


def _wkv_kernel(rf, vf, kkf, rb, vb, kkb, lwf, kdf, bbf, lwb, kdb, bbb, yf, yb, s_ref):
    @pl.when(pl.program_id(2) == 0)
    def _():
        s_ref[...] = jnp.zeros_like(s_ref)

    yf[...] = _wkv_chain(rf[...], vf[...], kkf[...], lwf[...], kdf[...], bbf[...], s_ref.at[0], False)
    yb[...] = _wkv_chain(rb[...], vb[...], kkb[...], lwb[...], kdb[...], bbb[...], s_ref.at[1], True)


def _wkv7_scan(r, v, kk, lw0, kd0, bb0, lw1, kd1, bb1):
    b, t, w = r.shape
    nc = t // CHUNK
    blk = (None, CHUNK, LANES)
    fwd = pl.BlockSpec(blk, lambda i, p, c: (i, c, p))
    bwd = pl.BlockSpec(blk, lambda i, p, c: (i, nc - 1 - c, p))
    return pl.pallas_call(
        _wkv_kernel,
        grid=(b, w // LANES, nc),
        in_specs=[fwd, fwd, fwd, bwd, bwd, bwd, fwd, fwd, fwd, bwd, bwd, bwd],
        out_specs=[fwd, bwd],
        out_shape=[jax.ShapeDtypeStruct((b, t, w), F32)] * 2,
        scratch_shapes=[pltpu.VMEM((2, LANES, LANES), F32)],
        compiler_params=_params("parallel", "parallel", "arbitrary"),
        name="wkv7_scan",
    )(r, v, kk, r, v, kk, lw0, kd0, bb0, lw1, kd1, bb1)


def _post_kernel(yf_ref, yb_ref, bonus_ref, g_ref, gg_ref, gb_ref, seg_ref, o_ref):
    seg = seg_ref[...]
    y = yf_ref[...] + yb_ref[...]
    m = _seg_sum(y, seg) * (1.0 / HEAD)
    yc = y - m
    var = _seg_sum(yc * yc, seg) * (1.0 / HEAD)
    yn = yc * lax.rsqrt(var + GN_EPS) * gg_ref[...] + gb_ref[...]
    o_ref[...] = ((yn + bonus_ref[...]) * g_ref[...]).astype(o_ref.dtype)


def _rwkv_post(yf, yb, bonus, g, gn_g, gn_b, seg, tm=512):
    m, w = yf.shape
    tile = pl.BlockSpec((tm, w), lambda i: (i, 0))
    row = pl.BlockSpec((1, w), lambda i: (0, 0))
    return pl.pallas_call(
        _post_kernel,
        grid=(m // tm,),
        in_specs=[tile, tile, tile, tile, row, row, pl.BlockSpec(seg.shape, lambda i: (0, 0))],
        out_specs=tile,
        out_shape=jax.ShapeDtypeStruct((m, w), BF16),
        compiler_params=_params("parallel"),
        name="rwkv_post",
    )(yf, yb, bonus, g, gn_g, gn_b, seg)


def _conv_kernel(gb_ref, gc_ref, h_ref, w_ref, o_ref):
    u = gc_ref[...] * h_ref[...]
    t = u.shape[0]
    rid = lax.broadcasted_iota(jnp.int32, u.shape, 0)
    up = jnp.where(rid == 0, 0.0, pltpu.roll(u, 1, axis=0))
    un = jnp.where(rid == t - 1, 0.0, pltpu.roll(u, t - 1, axis=0))
    hc = w_ref[0:1, :] * up + w_ref[1:2, :] * u + w_ref[2:3, :] * un
    o_ref[...] = (gb_ref[...] * hc).astype(o_ref.dtype)


def _shortconv(p, conv_w, tc=LANES):
    b, t, pw = p.shape
    w = conv_w.shape[1]
    nj = w // tc
    return pl.pallas_call(
        _conv_kernel,
        grid=(b, nj),
        in_specs=[pl.BlockSpec((None, t, tc), lambda bi, j: (bi, 0, j)),
                  pl.BlockSpec((None, t, tc), lambda bi, j: (bi, 0, nj + j)),
                  pl.BlockSpec((None, t, tc), lambda bi, j: (bi, 0, 2 * nj + j)),
                  pl.BlockSpec((conv_w.shape[0], tc), lambda bi, j: (0, j))],
        out_specs=pl.BlockSpec((None, t, tc), lambda bi, j: (bi, 0, j)),
        out_shape=jax.ShapeDtypeStruct((b, t, w), BF16),
        compiler_params=_params("parallel", "parallel"),
        name="shortconv",
    )(p, p, p, conv_w)


def _memkv_kernel(mem_ref, g_ref, b_ref, w_ref, o_ref):
    mn = _layer_norm(mem_ref[...], g_ref[...], b_ref[...])
    o_ref[...] = jnp.dot(mn.astype(BF16), w_ref[...], preferred_element_type=F32)


def _mem_kv(mem, g, b, w_kv, tn=512):
    m, d = mem.shape
    n = w_kv.shape[1]
    return pl.pallas_call(
        _memkv_kernel,
        grid=(n // tn,),
        in_specs=[pl.BlockSpec((m, d), lambda j: (0, 0)),
                  pl.BlockSpec((1, d), lambda j: (0, 0)),
                  pl.BlockSpec((1, d), lambda j: (0, 0)),
                  pl.BlockSpec((d, tn), lambda j: (0, j))],
        out_specs=pl.BlockSpec((m, tn), lambda j: (0, j)),
        out_shape=jax.ShapeDtypeStruct((m, n), F32),
        compiler_params=_params("parallel"),
        name="mem_kv",
    )(mem, g, b, w_kv)


def _xattn_kernel(q_ref, k_ref, v_ref, o_ref):
    s = _dot_nt(q_ref[...], k_ref[...]) * (XATTN_HEAD ** -0.5)
    e = jnp.exp(s - jnp.max(s, axis=-1, keepdims=True))
    attn = e / jnp.sum(e, axis=-1, keepdims=True)
    o_ref[...] = _dot(attn, v_ref[...]).astype(o_ref.dtype)


def _xattn(q, kv, tq=512):
    b, t, w = q.shape
    nh = w // XATTN_HEAD
    n_mem = kv.shape[1]
    return pl.pallas_call(
        _xattn_kernel,
        grid=(b, t // tq, nh),
        in_specs=[pl.BlockSpec((None, tq, XATTN_HEAD), lambda bi, i, h: (bi, i, h)),
                  pl.BlockSpec((None, n_mem, XATTN_HEAD), lambda bi, i, h: (bi, 0, h)),
                  pl.BlockSpec((None, n_mem, XATTN_HEAD), lambda bi, i, h: (bi, 0, nh + h))],
        out_specs=pl.BlockSpec((None, tq, XATTN_HEAD), lambda bi, i, h: (bi, i, h)),
        out_shape=jax.ShapeDtypeStruct((b, t, w), BF16),
        compiler_params=_params("parallel", "parallel", "parallel"),
        name="mem_xattn",
    )(q, kv, kv)


def _merge_kernel(alpha, x_ref, xb_ref, y0_ref, y1_ref, y2_ref, wbr_ref, wgt_ref, gb_ref, wo_ref,
                  g_ref, b_ref, o_ref, ob_ref, acc_ref):
    j = pl.program_id(1)

    @pl.when(j == 0)
    def _():
        acc_ref[...] = jnp.zeros_like(acc_ref)

    xb = xb_ref[...]
    mixed = None
    for n, y_ref in enumerate((y0_ref, y1_ref, y2_ref)):
        proj = jnp.dot(y_ref[...], wbr_ref[n], preferred_element_type=F32)
        gate = jax.nn.sigmoid(jnp.dot(xb, wgt_ref[n], preferred_element_type=F32) + gb_ref[n])
        mixed = gate * proj if mixed is None else mixed + gate * proj
    acc_ref[...] += jnp.dot(mixed.astype(BF16), wo_ref[...], preferred_element_type=F32)

    @pl.when(j == pl.num_programs(1) - 1)
    def _():
        y = _layer_norm(alpha * x_ref[...] + acc_ref[...], g_ref[...], b_ref[...])
        o_ref[...] = y
        ob_ref[...] = y.astype(BF16)


def _merge_block(x, xb, y0, y1, y2, w_branch, w_gate, gate_b, w_out, g, b, alpha, tm=256, tn=512):
    m, d = x.shape
    nb, bw, _ = w_branch.shape
    row_tile = lambda n: pl.BlockSpec((tm, n), lambda i, j: (i, 0))
    return pl.pallas_call(
        functools.partial(_merge_kernel, alpha),
        grid=(m // tm, d // tn),
        in_specs=[row_tile(d), row_tile(d), row_tile(bw), row_tile(bw), row_tile(bw),
                  pl.BlockSpec((nb, bw, tn), lambda i, j: (0, 0, j)),
                  pl.BlockSpec((nb, d, tn), lambda i, j: (0, 0, j)),
                  pl.BlockSpec((nb, 1, tn), lambda i, j: (0, 0, j)),
                  pl.BlockSpec((tn, d), lambda i, j: (j, 0)),
                  pl.BlockSpec((1, d), lambda i, j: (0, 0)),
                  pl.BlockSpec((1, d), lambda i, j: (0, 0))],
        out_specs=[row_tile(d), row_tile(d)],
        out_shape=[jax.ShapeDtypeStruct((m, d), F32), jax.ShapeDtypeStruct((m, d), BF16)],
        scratch_shapes=[pltpu.VMEM((tm, d), F32)],
        compiler_params=_params("parallel", "arbitrary"),
        name="merge_block",
    )(x, xb, y0, y1, y2, w_branch, w_gate, gate_b, w_out, g, b)


def _block_diag2(w):
    z = jnp.zeros_like(w[0])
    return jnp.concatenate([jnp.concatenate([w[0], z], axis=1),
                            jnp.concatenate([z, w[1]], axis=1)], axis=0)


def kernel(x, mem, ffn1_w_gate, ffn1_w_up, ffn1_w_down, ln1_g, ln1_b, w_in, rwkv_mu, rwkv_w0, rwkv_w_up, rwkv_a0, rwkv_a_up, rwkv_g_up, rwkv_k_k, rwkv_k_a, rwkv_r_k, rwkv_gn_g, rwkv_gn_b, conv_w, mem_ln_g, mem_ln_b, w_mem_kv, w_branch, gate_b, w_out, ln2_g, ln2_b, ffn2_w_gate, ffn2_w_up, ffn2_w_down, ln3_g, ln3_b):
    bsz, t, d = x.shape
    depth = w_in.shape[0]
    n_mem = mem.shape[1]
    rw = rwkv_k_k.shape[1]
    rproj = rwkv_mu.shape[1]
    cw = conv_w.shape[2]
    qw = w_mem_kv.shape[2] // 2
    nb = w_branch.shape[1]
    fdim = ffn1_w_gate.shape[2]
    ffn_tile = 512
    fpad = -fdim % ffn_tile
    alpha = (2 * depth) ** 0.25

    def ffn_weights(wg, wu, wd):
        wg = jnp.pad(wg, ((0, 0), (0, fpad))).astype(BF16)
        wu = jnp.pad(wu, ((0, 0), (0, fpad))).astype(BF16)
        wd = jnp.pad(wd, ((0, fpad), (0, 0))).astype(BF16)
        return wg, wu, wd

    row = lambda a: a.reshape(1, -1)
    head_id = jnp.arange(rw) // HEAD
    seg = (head_id[:, None] == head_id[None, :]).astype(BF16)
    xf = x.reshape(bsz * t, d)
    mem2 = mem.reshape(bsz * n_mem, d)

    for l in range(depth):
        wg, wu, wd = ffn_weights(ffn1_w_gate[l], ffn1_w_up[l], ffn1_w_down[l])
        xf, xb = _ffn_block(xf, wg, wu, wd, row(ln1_g[l]), row(ln1_b[l]), alpha, tf=ffn_tile)

        c0, c1, c2 = rproj, rproj + 3 * cw, rproj + 3 * cw + qw
        wl = w_in[l]
        p_rwkv = _matmul(xb, wl[:, :c0].astype(BF16), 512, 1152).reshape(bsz, t, c0)
        p_conv = _matmul(xb, wl[:, c0:c1].astype(BF16), 512, 1024).reshape(bsz, t, c1 - c0)
        q_mem = _matmul(xb, wl[:, c1:c2].astype(BF16), 512, 1024).reshape(bsz, t, qw)
        w_gate = wl[:, c2:].reshape(d, nb, d).transpose(1, 0, 2).astype(BF16)

        (r, v, kk, lw0, lw1, kd0, kd1, bb0, bb1, g, bonus) = _rwkv_prep(
            p_rwkv, row(rwkv_mu[l]), row(rwkv_w0[l]), _block_diag2(rwkv_w_up[l]),
            row(rwkv_a0[l]), _block_diag2(rwkv_a_up[l]), rwkv_g_up[l],
            row(rwkv_k_k[l]), row(rwkv_k_a[l]), row(rwkv_r_k[l]), seg)
        yf, yb = _wkv7_scan(r, v, kk, lw0, kd0, bb0, lw1, kd1, bb1)
        flat = lambda a: a.reshape(bsz * t, rw)
        y_rwkv = _rwkv_post(flat(yf), flat(yb), flat(bonus), flat(g),
                            row(rwkv_gn_g[l]), row(rwkv_gn_b[l]), seg)

        y_conv = _shortconv(p_conv, conv_w[l]).reshape(bsz * t, cw)

        kv = _mem_kv(mem2, row(mem_ln_g[l]), row(mem_ln_b[l]), w_mem_kv[l].astype(BF16))
        y_mem = _xattn(q_mem, kv.reshape(bsz, n_mem, 2 * qw)).reshape(bsz * t, qw)

        xf, xb = _merge_block(xf, xb, y_rwkv, y_conv, y_mem, w_branch[l].astype(BF16), w_gate,
                              gate_b[l].reshape(nb, 1, d), w_out[l].astype(BF16),
                              row(ln2_g[l]), row(ln2_b[l]), alpha)

        wg, wu, wd = ffn_weights(ffn2_w_gate[l], ffn2_w_up[l], ffn2_w_down[l])
        xf, xb = _ffn_block(xf, wg, wu, wd, row(ln3_g[l]), row(ln3_b[l]), alpha, tf=ffn_tile)
    return xf.reshape(bsz, t, d)
```

```python
import functools

import jax
import jax.numpy as jnp
from jax import lax
from jax.experimental import pallas as pl
from jax.experimental.pallas import tpu as pltpu

F32 = jnp.float32
BF16 = jnp.bfloat16

LANES = 128
SUBLANES = 8
HEAD = 64
CHUNK = 64
WKV_PAIRS = 4
XATTN_HEAD = 256
DECAY_SCALE = 0.606531
GN_EPS = 64e-5
LN_EPS = 1e-5
KK_EPS = 1e-12
VMEM_LIMIT = 56 * 1024 * 1024


def _params(*semantics):
    return pltpu.CompilerParams(dimension_semantics=semantics, vmem_limit_bytes=VMEM_LIMIT)


def _dot(a, b):
    return jnp.dot(a.astype(BF16), b.astype(BF16), preferred_element_type=F32)


def _dot_nt(a, b):
    return lax.dot_general(a.astype(BF16), b.astype(BF16), (((1,), (1,)), ((), ())),
                           preferred_element_type=F32)


def _dot_tn(a, b):
    return lax.dot_general(a.astype(BF16), b.astype(BF16), (((0,), (0,)), ((), ())),
                           preferred_element_type=F32)


def _split(x):
    hi = x.astype(BF16)
    lo = (x - hi.astype(F32)).astype(BF16)
    return hi, lo


def _dot_split(a, b):
    ah, al = _split(a)
    bh, bl = _split(b)
    return (jnp.dot(ah, bh, preferred_element_type=F32)
            + jnp.dot(ah, bl, preferred_element_type=F32)
            + jnp.dot(al, bh, preferred_element_type=F32))


def _seg_sum(x, seg):
    hi, lo = _split(x)
    return (jnp.dot(hi, seg, preferred_element_type=F32)
            + jnp.dot(lo, seg, preferred_element_type=F32))


def _layer_norm(z, g, b):
    mu = jnp.mean(z, axis=-1, keepdims=True)
    zc = z - mu
    var = jnp.mean(zc * zc, axis=-1, keepdims=True)
    return zc * lax.rsqrt(var + LN_EPS) * g + b


def _cast_cols_kernel(cuts, w_ref, *o_refs):
    for (lo, hi), o_ref in zip(cuts, o_refs):
        n = hi - lo
        o_ref[:, 0:n] = w_ref[:, lo:hi].astype(BF16)
        if o_ref.shape[1] > n:
            o_ref[:, n:] = jnp.zeros((o_ref.shape[0], o_ref.shape[1] - n), BF16)


def _cast_cols(w, l, cuts, widths, tr):
    _, rows, cols = w.shape
    return pl.pallas_call(
        functools.partial(_cast_cols_kernel, cuts),
        grid=(rows // tr,),
        in_specs=[pl.BlockSpec((None, tr, cols), lambda i: (l, i, 0))],
        out_specs=[pl.BlockSpec((tr, n), lambda i: (i, 0)) for n in widths],
        out_shape=[jax.ShapeDtypeStruct((rows, n), BF16) for n in widths],
        compiler_params=_params("parallel"),
        name="cast_cols",
    )(w)


def _cast_rows_kernel(n_valid, w_ref, o_ref):
    o_ref[...] = jnp.where(pl.program_id(0) < n_valid, w_ref[...], 0.0).astype(BF16)


def _cast_rows(w, l, rows_out, tr=LANES):
    _, rows, cols = w.shape
    n_valid = rows // tr
    return pl.pallas_call(
        functools.partial(_cast_rows_kernel, n_valid),
        grid=(rows_out // tr,),
        in_specs=[pl.BlockSpec((None, tr, cols), lambda i: (l, jnp.minimum(i, n_valid - 1), 0))],
        out_specs=pl.BlockSpec((tr, cols), lambda i: (i, 0)),
        out_shape=jax.ShapeDtypeStruct((rows_out, cols), BF16),
        compiler_params=_params("parallel"),
        name="cast_rows",
    )(w)


def _mm_kernel(x_ref, w_ref, o_ref):
    o_ref[...] = jnp.dot(x_ref[...], w_ref[...], preferred_element_type=F32).astype(o_ref.dtype)


def _matmul(x, w, tm, tn, out_dtype=F32):
    m, k = x.shape
    n = w.shape[1]
    return pl.pallas_call(
        _mm_kernel,
        grid=(n // tn, m // tm),
        in_specs=[pl.BlockSpec((tm, k), lambda j, i: (i, 0)),
                  pl.BlockSpec((k, tn), lambda j, i: (0, j))],
        out_specs=pl.BlockSpec((tm, tn), lambda j, i: (i, j)),
        out_shape=jax.ShapeDtypeStruct((m, n), out_dtype),
        compiler_params=_params("parallel", "parallel"),
        name="matmul",
    )(x, w)


def _ffn_kernel(alpha, x_ref, wg_ref, wu_ref, wd_ref, g_ref, b_ref, o_ref, ob_ref, xb_ref, acc_ref):
    f = pl.program_id(1)

    @pl.when(f == 0)
    def _():
        xb_ref[...] = x_ref[...].astype(BF16)
        acc_ref[...] = jnp.zeros_like(acc_ref)

    xb = xb_ref[...]
    hg = jnp.dot(xb, wg_ref[...], preferred_element_type=F32)
    hu = jnp.dot(xb, wu_ref[...], preferred_element_type=F32)
    h = (hg * jax.nn.sigmoid(hg)) * hu
    acc_ref[...] += jnp.dot(h.astype(BF16), wd_ref[...], preferred_element_type=F32)

    @pl.when(f == pl.num_programs(1) - 1)
    def _():
        y = _layer_norm(alpha * x_ref[...] + 0.5 * acc_ref[...], g_ref[...], b_ref[...])
        o_ref[...] = y
        ob_ref[...] = y.astype(BF16)


def _ffn_block(x, wg, wu, wd, g, b, alpha, tm=512, tf=512):
    m, d = x.shape
    fdim = wg.shape[1]
    return pl.pallas_call(
        functools.partial(_ffn_kernel, alpha),
        grid=(m // tm, fdim // tf),
        in_specs=[pl.BlockSpec((tm, d), lambda i, f: (i, 0)),
                  pl.BlockSpec((d, tf), lambda i, f: (0, f)),
                  pl.BlockSpec((d, tf), lambda i, f: (0, f)),
                  pl.BlockSpec((tf, d), lambda i, f: (f, 0)),
                  pl.BlockSpec((1, d), lambda i, f: (0, 0)),
                  pl.BlockSpec((1, d), lambda i, f: (0, 0))],
        out_specs=[pl.BlockSpec((tm, d), lambda i, f: (i, 0)),
                   pl.BlockSpec((tm, d), lambda i, f: (i, 0))],
        out_shape=[jax.ShapeDtypeStruct((m, d), F32), jax.ShapeDtypeStruct((m, d), BF16)],
        scratch_shapes=[pltpu.VMEM((tm, d), BF16), pltpu.VMEM((tm, d), F32)],
        compiler_params=_params("parallel", "arbitrary"),
        name="ffn_block",
    )(x, wg, wu, wd, g, b)


def _prep_kernel(width, p_ref, pp_ref, pn_ref, mu_ref, w0_ref, wup_ref, a0_ref, aup_ref, gup_ref,
                 kk_ref, ka_ref, rk_ref, seg_ref,
                 r_o, v_o, kk_o, lw0_o, lw1_o, kd0_o, kd1_o, bb0_o, bb1_o, g_o, bonus_o):
    i = pl.program_id(1)
    p = p_ref[...]
    tm = p.shape[0]
    prev_row = jnp.where(i > 0, pp_ref[SUBLANES - 1:SUBLANES, :], 0.0)
    next_row = jnp.where(i < pl.num_programs(1) - 1, pn_ref[0:1, :], 0.0)
    rid = lax.broadcasted_iota(jnp.int32, p.shape, 0)
    p_prev = jnp.where(rid == 0, prev_row, pltpu.roll(p, 1, axis=0))
    p_next = jnp.where(rid == tm - 1, next_row, pltpu.roll(p, tm - 1, axis=0))
    ps = p + mu_ref[...] * (0.5 * (p_prev + p_next) - p)

    w = width
    r = ps[:, 0:w]
    k = ps[:, w:2 * w]
    v = ps[:, 2 * w:3 * w]
    hw = ps[:, 3 * w:3 * w + LANES]
    ha = ps[:, 3 * w + LANES:3 * w + 2 * LANES]
    hg = ps[:, 3 * w + 2 * LANES:3 * w + 3 * LANES]

    lw = -DECAY_SCALE * jax.nn.sigmoid(w0_ref[...] + _dot_split(jnp.tanh(hw), wup_ref[...]))
    a = jax.nn.sigmoid(a0_ref[...] + _dot_split(ha, aup_ref[...]))
    g = _dot_split(jax.nn.sigmoid(hg), gup_ref[...])
    seg = seg_ref[...]
    kkr = k * kk_ref[...]
    kk = kkr * lax.rsqrt(_seg_sum(kkr * kkr, seg) + KK_EPS)
    a0 = a[:, 0:w]
    a1 = a[:, w:2 * w]
    ka = ka_ref[...]
    kd0 = k * (1.0 + (a0 - 1.0) * ka)
    kd1 = k * (1.0 + (a1 - 1.0) * ka)
    bonus = _seg_sum(r * (0.5 * (kd0 + kd1)) * rk_ref[...], seg) * v

    r_o[...] = r
    v_o[...] = v
    kk_o[...] = kk
    lw0_o[...] = lw[:, 0:w]
    lw1_o[...] = lw[:, w:2 * w]
    kd0_o[...] = kd0
    kd1_o[...] = kd1
    bb0_o[...] = kk * a0
    bb1_o[...] = kk * a1
    g_o[...] = g
    bonus_o[...] = bonus


def _rwkv_prep(p, mu, w0, wup, a0, aup, gup, k_k, k_a, r_k, seg, tm=256):
    b, t, pw = p.shape
    w = k_k.shape[1]
    nblk = t // SUBLANES
    per = tm // SUBLANES
    row = lambda n: pl.BlockSpec((1, n), lambda bi, i: (0, 0))
    full = lambda a: pl.BlockSpec(a.shape, lambda bi, i: (0, 0))
    out_spec = pl.BlockSpec((None, tm, w), lambda bi, i: (bi, i, 0))
    return pl.pallas_call(
        functools.partial(_prep_kernel, w),
        grid=(b, t // tm),
        in_specs=[pl.BlockSpec((None, tm, pw), lambda bi, i: (bi, i, 0)),
                  pl.BlockSpec((None, SUBLANES, pw), lambda bi, i: (bi, jnp.maximum(i * per - 1, 0), 0)),
                  pl.BlockSpec((None, SUBLANES, pw), lambda bi, i: (bi, jnp.minimum((i + 1) * per, nblk - 1), 0)),
                  row(pw), row(2 * w), full(wup), row(2 * w), full(aup), full(gup),
                  row(w), row(w), row(w), full(seg)],
        out_specs=[out_spec] * 11,
        out_shape=[jax.ShapeDtypeStruct((b, t, w), F32)] * 11,
        compiler_params=_params("parallel", "parallel"),
        name="rwkv_prep",
    )(p, p, p, mu, w0, wup, a0, aup, gup, k_k, k_a, r_k, seg)


def _wkv_masks(reverse):
    c = CHUNK
    n2 = 2 * c
    ti = lax.broadcasted_iota(jnp.int32, (c, c), 0)
    si = lax.broadcasted_iota(jnp.int32, (c, c), 1)
    cmat = jnp.where((si >= ti) if reverse else (si <= ti), 1.0, 0.0).astype(BF16)
    ii = lax.broadcasted_iota(jnp.int32, (n2, n2), 0)
    jj = lax.broadcasted_iota(jnp.int32, (n2, n2), 1)
    same = (ii < c) == (jj < c)
    strict = same & ((jj > ii) if reverse else (jj < ii))
    incl = same & ((jj >= ii) if reverse else (jj <= ii))
    return cmat, strict, incl


def _wkv_chains(chains):
    c = CHUNK
    n2 = 2 * c
    head0 = lax.broadcasted_iota(jnp.int32, (c, LANES), 1) < HEAD
    ii = lax.broadcasted_iota(jnp.int32, (n2, n2), 0)
    jj = lax.broadcasted_iota(jnp.int32, (n2, n2), 1)
    eye = jnp.where(ii == jj, 1.0, 0.0)
    masks = {rev: _wkv_masks(rev) for rev in sorted({ch[7] for ch in chains})}
    each = lambda f, *cols: [f(*xs) for xs in zip(*cols)]

    def stack(x):
        return jnp.concatenate([jnp.where(head0, x, 0.0), jnp.where(head0, 0.0, x)], axis=0)

    r, v, kk, lw, kd, bb, s_refs, rev = [list(col) for col in zip(*chains)]
    cmat = [masks[q][0] for q in rev]
    strict = [masks[q][1] for q in rev]
    incl = [masks[q][2] for q in rev]

    def cumsum(cm, x):
        hi = x.astype(BF16)
        rem = x - hi.astype(F32)
        mid = rem.astype(BF16)
        lo = (rem - mid.astype(F32)).astype(BF16)
        return (jnp.dot(cm, hi, preferred_element_type=F32)
                + jnp.dot(cm, mid, preferred_element_type=F32)
                + jnp.dot(cm, lo, preferred_element_type=F32))

    cum = each(cumsum, cmat, lw)
    tot = each(lambda x, q: x[0:1, :] if q else x[c - 1:c, :], cum, rev)
    e_neg = each(lambda x: jnp.exp(-x), cum)
    e_tot = each(lambda tt, x: jnp.exp(tt - x), tot, cum)
    kkt = each(lambda a, x, l: stack(a * jnp.exp(x - l)), kk, cum, lw)
    rt = each(lambda a, x: stack(a * jnp.exp(x)), r, cum)
    v2 = each(stack, v)
    bh = each(jnp.multiply, bb, e_neg)
    kh = each(jnp.multiply, kd, e_neg)
    bp2 = each(lambda a, e: stack(a * e), bb, e_tot)
    kp2 = each(lambda a, e: stack(a * e), kd, e_tot)

    g = each(lambda a, b_, c_, d_: _dot_nt(jnp.concatenate([a, b_], axis=0),
                                           jnp.concatenate([c_, c_, d_, d_], axis=0)), kkt, rt, bh, kh)
    a_bd = each(lambda m, x: jnp.where(m, x[0:n2, 0:n2], 0.0), strict, g)
    bm_bd = each(lambda m, x: jnp.where(m, x[0:n2, n2:2 * n2], 0.0), strict, g)
    mqb_bd = each(lambda m, x: jnp.where(m, x[n2:2 * n2, 0:n2], 0.0), incl, g)
    mqk_bd = each(lambda m, x: jnp.where(m, x[n2:2 * n2, n2:2 * n2], 0.0), incl, g)

    x = each(lambda a: eye - a, a_bd)
    ap = a_bd
    for _ in range((c - 1).bit_length() - 1):
        ap = each(lambda a: _dot(a, a), ap)
        x = each(lambda a, p: a + _dot(a, p), x, ap)

    bv = each(_dot, bm_bd, v2)
    w12 = each(lambda a, k_, b_: _dot(a, jnp.concatenate([k_, b_], axis=1)), x, kkt, bv)
    qw = each(_dot, mqb_bd, w12)
    g1 = each(lambda a, q: a - q[:, 0:LANES], rt, qw)
    g2 = each(lambda m, a, q: _dot(m, a) - q[:, LANES:2 * LANES], mqk_bd, v2, qw)
    wb = each(_dot_tn, w12, bp2)
    m_bd = each(lambda tt, w_: eye * jnp.exp(tt) - w_[0:LANES], tot, wb)
    n_bd = each(lambda a, k_, w_: _dot_tn(a, k_) - w_[LANES:2 * LANES], v2, kp2, wb)

    s0 = [s_ref[...] for s_ref in s_refs]
    y2 = each(lambda a, s, b_: _dot_nt(a, s) + b_, g1, s0, g2)
    s1 = each(lambda s, m, n: _dot(s, m) + n, s0, m_bd, n_bd)
    for s_ref, s in zip(s_refs, s1):
        s_ref[...] = s
    return each(lambda y: y[0:c] + y[c:n2], y2)


def _wkv_kernel(rf, vf, kkf, rb, vb, kkb, lwf, kdf, bbf, lwb, kdb, bbb, yf, yb, s_ref):
    @pl.when(pl.program_id(2) == 0)
    def _():
        s_ref[...] = jnp.zeros_like(s_ref)

    chains = []
    slices = [slice(j * LANES, (j + 1) * LANES) for j in range(rf.shape[1] // LANES)]
    for j, sl in enumerate(slices):
        chains.append((rf[:, sl], vf[:, sl], kkf[:, sl], lwf[:, sl], kdf[:, sl], bbf[:, sl],
                       s_ref.at[0, j], False))
        chains.append((rb[:, sl], vb[:, sl], kkb[:, sl], lwb[:, sl], kdb[:, sl], bbb[:, sl],
                       s_ref.at[1, j], True))
    ys = _wkv_chains(chains)
    for j, sl in enumerate(slices):
        yf[:, sl] = ys[2 * j]
        yb[:, sl] = ys[2 * j + 1]


def _wkv7_scan(r, v, kk, lw0, kd0, bb0, lw1, kd1, bb1, pairs=WKV_PAIRS):
    b, t, w = r.shape
    nc = t // CHUNK
    lw_blk = pairs * LANES
    blk = (None, CHUNK, lw_blk)
    fwd = pl.BlockSpec(blk, lambda i, p, c: (i, c, p))
    bwd = pl.BlockSpec(blk, lambda i, p, c: (i, nc - 1 - c, p))
    return pl.pallas_call(
        _wkv_kernel,
        grid=(b, w // lw_blk, nc),
        in_specs=[fwd, fwd, fwd, bwd, bwd, bwd, fwd, fwd, fwd, bwd, bwd, bwd],
        out_specs=[fwd, bwd],
        out_shape=[jax.ShapeDtypeStruct((b, t, w), F32)] * 2,
        scratch_shapes=[pltpu.VMEM((2, pairs, LANES, LANES), F32)],
        compiler_params=_params("parallel", "parallel", "arbitrary"),
        name="wkv7_scan",
    )(r, v, kk, r, v, kk, lw0, kd0, bb0, lw1, kd1, bb1)


def _post_kernel(yf_ref, yb_ref, bonus_ref, g_ref, gg_ref, gb_ref, seg_ref, o_ref):
    seg = seg_ref[...]
    y = yf_ref[...] + yb_ref[...]
    m = _seg_sum(y, seg) * (1.0 / HEAD)
    yc = y - m
    var = _seg_sum(yc * yc, seg) * (1.0 / HEAD)
    yn = yc * lax.rsqrt(var + GN_EPS) * gg_ref[...] + gb_ref[...]
    o_ref[...] = ((yn + bonus_ref[...]) * g_ref[...]).astype(o_ref.dtype)


def _rwkv_post(yf, yb, bonus, g, gn_g, gn_b, seg, tm=512):
    m, w = yf.shape
    tile = pl.BlockSpec((tm, w), lambda i: (i, 0))
    row = pl.BlockSpec((1, w), lambda i: (0, 0))
    return pl.pallas_call(
        _post_kernel,
        grid=(m // tm,),
        in_specs=[tile, tile, tile, tile, row, row, pl.BlockSpec(seg.shape, lambda i: (0, 0))],
        out_specs=tile,
        out_shape=jax.ShapeDtypeStruct((m, w), BF16),
        compiler_params=_params("parallel"),
        name="rwkv_post",
    )(yf, yb, bonus, g, gn_g, gn_b, seg)


def _conv_kernel(gb_ref, gc_ref, h_ref, w_ref, o_ref):
    u = gc_ref[...] * h_ref[...]
    t = u.shape[0]
    rid = lax.broadcasted_iota(jnp.int32, u.shape, 0)
    up = jnp.where(rid == 0, 0.0, pltpu.roll(u, 1, axis=0))
    un = jnp.where(rid == t - 1, 0.0, pltpu.roll(u, t - 1, axis=0))
    hc = w_ref[0:1, :] * up + w_ref[1:2, :] * u + w_ref[2:3, :] * un
    o_ref[...] = (gb_ref[...] * hc).astype(o_ref.dtype)


def _shortconv(p, conv_w, tc=LANES):
    b, t, pw = p.shape
    w = conv_w.shape[1]
    nj = w // tc
    return pl.pallas_call(
        _conv_kernel,
        grid=(b, nj),
        in_specs=[pl.BlockSpec((None, t, tc), lambda bi, j: (bi, 0, j)),
                  pl.BlockSpec((None, t, tc), lambda bi, j: (bi, 0, nj + j)),
                  pl.BlockSpec((None, t, tc), lambda bi, j: (bi, 0, 2 * nj + j)),
                  pl.BlockSpec((conv_w.shape[0], tc), lambda bi, j: (0, j))],
        out_specs=pl.BlockSpec((None, t, tc), lambda bi, j: (bi, 0, j)),
        out_shape=jax.ShapeDtypeStruct((b, t, w), BF16),
        compiler_params=_params("parallel", "parallel"),
        name="shortconv",
    )(p, p, p, conv_w)


def _memkv_kernel(mem_ref, g_ref, b_ref, w_ref, o_ref):
    mn = _layer_norm(mem_ref[...], g_ref[...], b_ref[...])
    o_ref[...] = jnp.dot(mn.astype(BF16), w_ref[...].astype(BF16), preferred_element_type=F32)


def _mem_kv(mem, g, b, w_kv, l, tn=512):
    m, d = mem.shape
    n = w_kv.shape[2]
    return pl.pallas_call(
        _memkv_kernel,
        grid=(n // tn,),
        in_specs=[pl.BlockSpec((m, d), lambda j: (0, 0)),
                  pl.BlockSpec((1, d), lambda j: (0, 0)),
                  pl.BlockSpec((1, d), lambda j: (0, 0)),
                  pl.BlockSpec((None, d, tn), lambda j: (l, 0, j))],
        out_specs=pl.BlockSpec((m, tn), lambda j: (0, j)),
        out_shape=jax.ShapeDtypeStruct((m, n), F32),
        compiler_params=_params("parallel"),
        name="mem_kv",
    )(mem, g, b, w_kv)


def _xattn_kernel(q_ref, k_ref, v_ref, o_ref):
    s = _dot_nt(q_ref[...], k_ref[...]) * (XATTN_HEAD ** -0.5)
    e = jnp.exp(s - jnp.max(s, axis=-1, keepdims=True))
    attn = e / jnp.sum(e, axis=-1, keepdims=True)
    o_ref[...] = _dot(attn, v_ref[...]).astype(o_ref.dtype)


def _xattn(q, kv, tq=512):
    b, t, w = q.shape
    nh = w // XATTN_HEAD
    n_mem = kv.shape[1]
    return pl.pallas_call(
        _xattn_kernel,
        grid=(b, t // tq, nh),
        in_specs=[pl.BlockSpec((None, tq, XATTN_HEAD), lambda bi, i, h: (bi, i, h)),
                  pl.BlockSpec((None, n_mem, XATTN_HEAD), lambda bi, i, h: (bi, 0, h)),
                  pl.BlockSpec((None, n_mem, XATTN_HEAD), lambda bi, i, h: (bi, 0, nh + h))],
        out_specs=pl.BlockSpec((None, tq, XATTN_HEAD), lambda bi, i, h: (bi, i, h)),
        out_shape=jax.ShapeDtypeStruct((b, t, w), BF16),
        compiler_params=_params("parallel", "parallel", "parallel"),
        name="mem_xattn",
    )(q, kv, kv)


def _merge_kernel(alpha, x_ref, xb_ref, y0_ref, y1_ref, y2_ref, wbr_ref, wg0_ref, wg1_ref, wg2_ref,
                  gb_ref, wo_ref, g_ref, b_ref, o_ref, ob_ref, acc_ref):
    j = pl.program_id(1)

    @pl.when(j == 0)
    def _():
        acc_ref[...] = jnp.zeros_like(acc_ref)

    xb = xb_ref[...]
    mixed = None
    for n, (y_ref, wg_ref) in enumerate(((y0_ref, wg0_ref), (y1_ref, wg1_ref), (y2_ref, wg2_ref))):
        proj = jnp.dot(y_ref[...], wbr_ref[n], preferred_element_type=F32)
        gate = jax.nn.sigmoid(jnp.dot(xb, wg_ref[...], preferred_element_type=F32) + gb_ref[n])
        mixed = gate * proj if mixed is None else mixed + gate * proj
    acc_ref[...] += jnp.dot(mixed.astype(BF16), wo_ref[...], preferred_element_type=F32)

    @pl.when(j == pl.num_programs(1) - 1)
    def _():
        y = _layer_norm(alpha * x_ref[...] + acc_ref[...], g_ref[...], b_ref[...])
        o_ref[...] = y
        ob_ref[...] = y.astype(BF16)


def _merge_block(x, xb, y0, y1, y2, w_branch, w_gate, gate_b, w_out, g, b, alpha, tm=256, tn=512):
    m, d = x.shape
    nb, bw, _ = w_branch.shape
    row_tile = lambda n: pl.BlockSpec((tm, n), lambda i, j: (i, 0))
    gate_cols = lambda n: pl.BlockSpec((d, tn), lambda i, j: (0, n * (d // tn) + j))
    return pl.pallas_call(
        functools.partial(_merge_kernel, alpha),
        grid=(m // tm, d // tn),
        in_specs=[row_tile(d), row_tile(d), row_tile(bw), row_tile(bw), row_tile(bw),
                  pl.BlockSpec((nb, bw, tn), lambda i, j: (0, 0, j)),
                  gate_cols(0), gate_cols(1), gate_cols(2),
                  pl.BlockSpec((nb, 1, tn), lambda i, j: (0, 0, j)),
                  pl.BlockSpec((tn, d), lambda i, j: (j, 0)),
                  pl.BlockSpec((1, d), lambda i, j: (0, 0)),
                  pl.BlockSpec((1, d), lambda i, j: (0, 0))],
        out_specs=[row_tile(d), row_tile(d)],
        out_shape=[jax.ShapeDtypeStruct((m, d), F32), jax.ShapeDtypeStruct((m, d), BF16)],
        scratch_shapes=[pltpu.VMEM((tm, d), F32)],
        compiler_params=_params("parallel", "arbitrary"),
        name="merge_block",
    )(x, xb, y0, y1, y2, w_branch, w_gate, w_gate, w_gate, gate_b, w_out, g, b)


def _block_diag2(w):
    z = jnp.zeros_like(w[0])
    return jnp.concatenate([jnp.concatenate([w[0], z], axis=1),
                            jnp.concatenate([z, w[1]], axis=1)], axis=0)


def kernel(x, mem, ffn1_w_gate, ffn1_w_up, ffn1_w_down, ln1_g, ln1_b, w_in, rwkv_mu, rwkv_w0, rwkv_w_up, rwkv_a0, rwkv_a_up, rwkv_g_up, rwkv_k_k, rwkv_k_a, rwkv_r_k, rwkv_gn_g, rwkv_gn_b, conv_w, mem_ln_g, mem_ln_b, w_mem_kv, w_branch, gate_b, w_out, ln2_g, ln2_b, ffn2_w_gate, ffn2_w_up, ffn2_w_down, ln3_g, ln3_b):
    bsz, t, d = x.shape
    depth = w_in.shape[0]
    n_mem = mem.shape[1]
    rw = rwkv_k_k.shape[1]
    rproj = rwkv_mu.shape[1]
    cw = conv_w.shape[2]
    qw = w_mem_kv.shape[2] // 2
    nb = w_branch.shape[1]
    fdim = ffn1_w_gate.shape[2]
    ffn_tile = 512
    fpad = fdim + (-fdim % ffn_tile)
    alpha = (2 * depth) ** 0.25

    def ffn_weights(wg, wu, wd, l):
        wg = _cast_cols(wg, l, ((0, fdim),), (fpad,), 256)[0]
        wu = _cast_cols(wu, l, ((0, fdim),), (fpad,), 256)[0]
        return wg, wu, _cast_rows(wd, l, fpad)

    row = lambda a: a.reshape(1, -1)
    head_id = jnp.arange(rw) // HEAD
    seg = (head_id[:, None] == head_id[None, :]).astype(BF16)
    xf = x.reshape(bsz * t, d)
    mem2 = mem.reshape(bsz * n_mem, d)
    c0, c1, c2, c3 = rproj, rproj + 3 * cw, rproj + 3 * cw + qw, w_in.shape[2]
    in_cuts = ((0, c0), (c0, c1), (c1, c2), (c2, c3))
    in_widths = tuple(hi - lo for lo, hi in in_cuts)
    w_branch2 = w_branch.reshape(depth, nb * w_branch.shape[2], d)

    for l in range(depth):
        wg, wu, wd = ffn_weights(ffn1_w_gate, ffn1_w_up, ffn1_w_down, l)
        xf, xb = _ffn_block(xf, wg, wu, wd, row(ln1_g[l]), row(ln1_b[l]), alpha, tf=ffn_tile)

        w_r, w_c, w_q, w_gate = _cast_cols(w_in, l, in_cuts, in_widths, 128)
        p_rwkv = _matmul(xb, w_r, 512, 1152).reshape(bsz, t, c0)
        p_conv = _matmul(xb, w_c, 512, 1024).reshape(bsz, t, c1 - c0)
        q_mem = _matmul(xb, w_q, 512, 1024).reshape(bsz, t, qw)

        (r, v, kk, lw0, lw1, kd0, kd1, bb0, bb1, g, bonus) = _rwkv_prep(
            p_rwkv, row(rwkv_mu[l]), row(rwkv_w0[l]), _block_diag2(rwkv_w_up[l]),
            row(rwkv_a0[l]), _block_diag2(rwkv_a_up[l]), rwkv_g_up[l],
            row(rwkv_k_k[l]), row(rwkv_k_a[l]), row(rwkv_r_k[l]), seg)
        yf, yb = _wkv7_scan(r, v, kk, lw0, kd0, bb0, lw1, kd1, bb1)
        flat = lambda a: a.reshape(bsz * t, rw)
        y_rwkv = _rwkv_post(flat(yf), flat(yb), flat(bonus), flat(g),
                            row(rwkv_gn_g[l]), row(rwkv_gn_b[l]), seg)

        y_conv = _shortconv(p_conv, conv_w[l]).reshape(bsz * t, cw)

        kv = _mem_kv(mem2, row(mem_ln_g[l]), row(mem_ln_b[l]), w_mem_kv, l)
        y_mem = _xattn(q_mem, kv.reshape(bsz, n_mem, 2 * qw)).reshape(bsz * t, qw)

        w_br = _cast_cols(w_branch2, l, ((0, d),), (d,), 256)[0].reshape(nb, -1, d)
        w_o = _cast_cols(w_out, l, ((0, d),), (d,), 256)[0]
        xf, xb = _merge_block(xf, xb, y_rwkv, y_conv, y_mem, w_br, w_gate,
                              gate_b[l].reshape(nb, 1, d), w_o, row(ln2_g[l]), row(ln2_b[l]), alpha)

        wg, wu, wd = ffn_weights(ffn2_w_gate, ffn2_w_up, ffn2_w_down, l)
        xf, xb = _ffn_block(xf, wg, wu, wd, row(ln3_g[l]), row(ln3_b[l]), alpha, tf=ffn_tile)
    return xf.reshape(bsz, t, d)
```

```python
import functools

import jax
import jax.numpy as jnp
from jax import lax
from jax.experimental import pallas as pl
from jax.experimental.pallas import tpu as pltpu

F32 = jnp.float32
BF16 = jnp.bfloat16

LANES = 128
MXU_DIM = 256
SUBLANES = 8
HEAD = 64
CHUNK = 64
WKV_PAIRS = 8
XATTN_HEAD = 256
DECAY_SCALE = 0.606531
GN_EPS = 64e-5
LN_EPS = 1e-5
KK_EPS = 1e-12
VMEM_LIMIT = 60 * 1024 * 1024


def _params(*semantics):
    return pltpu.CompilerParams(dimension_semantics=semantics, vmem_limit_bytes=VMEM_LIMIT)


def _dot(a, b):
    return jnp.dot(a.astype(BF16), b.astype(BF16), preferred_element_type=F32)


def _dot_nt(a, b):
    return lax.dot_general(a.astype(BF16), b.astype(BF16), (((1,), (1,)), ((), ())),
                           preferred_element_type=F32)


def _dot_tn(a, b):
    return lax.dot_general(a.astype(BF16), b.astype(BF16), (((0,), (0,)), ((), ())),
                           preferred_element_type=F32)


def _split(x):
    hi = x.astype(BF16)
    lo = (x - hi.astype(F32)).astype(BF16)
    return hi, lo


def _dot_split(a, b):
    ah, al = _split(a)
    bh, bl = _split(b)
    return (jnp.dot(ah, bh, preferred_element_type=F32)
            + jnp.dot(ah, bl, preferred_element_type=F32)
            + jnp.dot(al, bh, preferred_element_type=F32))


def _seg_sum(x, seg):
    hi, lo = _split(x)
    gw = seg.shape[0]
    parts = [jnp.dot(hi[:, j:j + gw], seg, preferred_element_type=F32)
             + jnp.dot(lo[:, j:j + gw], seg, preferred_element_type=F32)
             for j in range(0, x.shape[1], gw)]
    return jnp.concatenate(parts, axis=1)


def _layer_norm(z, g, b):
    mu = jnp.mean(z, axis=-1, keepdims=True)
    zc = z - mu
    var = jnp.mean(zc * zc, axis=-1, keepdims=True)
    return zc * lax.rsqrt(var + LN_EPS) * g + b


def _cast_cols_kernel(cuts, w_ref, *o_refs):
    for (lo, hi), o_ref in zip(cuts, o_refs):
        n = hi - lo
        o_ref[:, 0:n] = w_ref[:, lo:hi].astype(BF16)
        if o_ref.shape[1] > n:
            o_ref[:, n:] = jnp.zeros((o_ref.shape[0], o_ref.shape[1] - n), BF16)


def _cast_cols(w, l, cuts, widths, tr):
    _, rows, cols = w.shape
    return pl.pallas_call(
        functools.partial(_cast_cols_kernel, cuts),
        grid=(rows // tr,),
        in_specs=[pl.BlockSpec((None, tr, cols), lambda i: (l, i, 0))],
        out_specs=[pl.BlockSpec((tr, n), lambda i: (i, 0)) for n in widths],
        out_shape=[jax.ShapeDtypeStruct((rows, n), BF16) for n in widths],
        compiler_params=_params("parallel"),
        name="cast_cols",
    )(w)


def _cast_rows_kernel(n_valid, w_ref, o_ref):
    o_ref[...] = jnp.where(pl.program_id(0) < n_valid, w_ref[...], 0.0).astype(BF16)


def _cast_rows(w, l, rows_out, tr):
    _, rows, cols = w.shape
    n_valid = rows // tr
    return pl.pallas_call(
        functools.partial(_cast_rows_kernel, n_valid),
        grid=(pl.cdiv(rows_out, tr),),
        in_specs=[pl.BlockSpec((None, tr, cols), lambda i: (l, jnp.minimum(i, n_valid - 1), 0))],
        out_specs=pl.BlockSpec((tr, cols), lambda i: (i, 0)),
        out_shape=jax.ShapeDtypeStruct((rows_out, cols), BF16),
        compiler_params=_params("parallel"),
        name="cast_rows",
    )(w)


def _mm_kernel(x_ref, w_ref, o_ref):
    o_ref[...] = jnp.dot(x_ref[...], w_ref[...], preferred_element_type=F32).astype(o_ref.dtype)


def _matmul(x, w, tm, tn, out_dtype=F32):
    m, k = x.shape
    n = w.shape[1]
    w_mode = pl.Buffered(1) if n == tn else None
    return pl.pallas_call(
        _mm_kernel,
        grid=(n // tn, m // tm),
        in_specs=[pl.BlockSpec((tm, k), lambda j, i: (i, 0)),
                  pl.BlockSpec((k, tn), lambda j, i: (0, j), pipeline_mode=w_mode)],
        out_specs=pl.BlockSpec((tm, tn), lambda j, i: (i, j)),
        out_shape=jax.ShapeDtypeStruct((m, n), out_dtype),
        compiler_params=_params("parallel", "parallel"),
        name="matmul",
    )(x, w)


def _ln_epilogue(alpha, scale, x_ref, g_ref, b_ref, o_ref, ob_ref, rows=256):
    for r0 in range(0, o_ref.shape[0], rows):
        sl = slice(r0, r0 + rows)
        y = _layer_norm(alpha * x_ref[sl, :] + scale * o_ref[sl, :], g_ref[...], b_ref[...])
        o_ref[sl, :] = y
        ob_ref[sl, :] = y.astype(BF16)


def _ffn_kernel(alpha, x_ref, wg_ref, wu_ref, wd_ref, g_ref, b_ref, o_ref, ob_ref, xb_ref):
    f = pl.program_id(1)
    tf = wd_ref.shape[0]

    @pl.when(f == 0)
    def _():
        xb_ref[...] = x_ref[...].astype(BF16)
        o_ref[...] = jnp.zeros_like(o_ref)

    xb = xb_ref[...]
    hg = jnp.dot(xb, wg_ref[...], preferred_element_type=F32)
    hu = jnp.dot(xb, wu_ref[...], preferred_element_type=F32)
    h = ((hg * jax.nn.sigmoid(hg)) * hu).astype(BF16)
    for n0 in range(0, o_ref.shape[1], tf):
        o_ref[:, n0:n0 + tf] += jnp.dot(h, wd_ref[:, n0:n0 + tf], preferred_element_type=F32)

    @pl.when(f == pl.num_programs(1) - 1)
    def _():
        _ln_epilogue(alpha, 0.5, x_ref, g_ref, b_ref, o_ref, ob_ref)


def _ffn_block(x, wg, wu, wd, g, b, alpha, tm=1024, tf=512):
    m, d = x.shape
    fdim = wg.shape[1]
    return pl.pallas_call(
        functools.partial(_ffn_kernel, alpha),
        grid=(m // tm, fdim // tf),
        in_specs=[pl.BlockSpec((tm, d), lambda i, f: (i, 0), pipeline_mode=pl.Buffered(1)),
                  pl.BlockSpec((d, tf), lambda i, f: (0, f)),
                  pl.BlockSpec((d, tf), lambda i, f: (0, f)),
                  pl.BlockSpec((tf, d), lambda i, f: (f, 0)),
                  pl.BlockSpec((1, d), lambda i, f: (0, 0)),
                  pl.BlockSpec((1, d), lambda i, f: (0, 0))],
        out_specs=[pl.BlockSpec((tm, d), lambda i, f: (i, 0)),
                   pl.BlockSpec((tm, d), lambda i, f: (i, 0))],
        out_shape=[jax.ShapeDtypeStruct((m, d), F32), jax.ShapeDtypeStruct((m, d), BF16)],
        scratch_shapes=[pltpu.VMEM((tm, d), BF16)],
        compiler_params=_params("parallel", "arbitrary"),
        name="ffn_block",
    )(x, wg, wu, wd, g, b)


def _prep_kernel(width, p_ref, pp_ref, pn_ref, mu_ref, w0_ref, wup_ref, a0_ref, aup_ref, gup_ref,
                 kk_ref, ka_ref, rk_ref, seg_ref,
                 r_o, v_o, kk_o, lw0_o, lw1_o, kd0_o, kd1_o, bb0_o, bb1_o, g_o, bonus_o):
    i = pl.program_id(1)
    p = p_ref[...]
    tm = p.shape[0]
    prev_row = jnp.where(i > 0, pp_ref[SUBLANES - 1:SUBLANES, :], 0.0)
    next_row = jnp.where(i < pl.num_programs(1) - 1, pn_ref[0:1, :], 0.0)
    rid = lax.broadcasted_iota(jnp.int32, p.shape, 0)
    p_prev = jnp.where(rid == 0, prev_row, pltpu.roll(p, 1, axis=0))
    p_next = jnp.where(rid == tm - 1, next_row, pltpu.roll(p, tm - 1, axis=0))
    ps = p + mu_ref[...] * (0.5 * (p_prev + p_next) - p)

    w = width
    r = ps[:, 0:w]
    k = ps[:, w:2 * w]
    v = ps[:, 2 * w:3 * w]
    hw = ps[:, 3 * w:3 * w + LANES]
    ha = ps[:, 3 * w + LANES:3 * w + 2 * LANES]
    hg = ps[:, 3 * w + 2 * LANES:3 * w + 3 * LANES]

    lw = -DECAY_SCALE * jax.nn.sigmoid(w0_ref[...] + _dot_split(jnp.tanh(hw), wup_ref[...]))
    a = jax.nn.sigmoid(a0_ref[...] + _dot_split(ha, aup_ref[...]))
    g = _dot_split(jax.nn.sigmoid(hg), gup_ref[...])
    seg = seg_ref[...]
    kkr = k * kk_ref[...]
    kk = kkr * lax.rsqrt(_seg_sum(kkr * kkr, seg) + KK_EPS)
    a0 = a[:, 0:w]
    a1 = a[:, w:2 * w]
    ka = ka_ref[...]
    kd0 = k * (1.0 + (a0 - 1.0) * ka)
    kd1 = k * (1.0 + (a1 - 1.0) * ka)
    bonus = _seg_sum(r * (0.5 * (kd0 + kd1)) * rk_ref[...], seg) * v

    r_o[...] = r
    v_o[...] = v
    kk_o[...] = kk
    lw0_o[...] = lw[:, 0:w]
    lw1_o[...] = lw[:, w:2 * w]
    kd0_o[...] = kd0
    kd1_o[...] = kd1
    bb0_o[...] = kk * a0
    bb1_o[...] = kk * a1
    g_o[...] = g
    bonus_o[...] = bonus


def _rwkv_prep(p, mu, w0, wup, a0, aup, gup, k_k, k_a, r_k, seg, tm=256):
    b, t, pw = p.shape
    w = k_k.shape[1]
    nblk = t // SUBLANES
    per = tm // SUBLANES
    row = lambda n: pl.BlockSpec((1, n), lambda bi, i: (0, 0))
    full = lambda a: pl.BlockSpec(a.shape, lambda bi, i: (0, 0))
    out_spec = pl.BlockSpec((None, tm, w), lambda bi, i: (bi, i, 0))
    return pl.pallas_call(
        functools.partial(_prep_kernel, w),
        grid=(b, t // tm),
        in_specs=[pl.BlockSpec((None, tm, pw), lambda bi, i: (bi, i, 0)),
                  pl.BlockSpec((None, SUBLANES, pw), lambda bi, i: (bi, jnp.maximum(i * per - 1, 0), 0)),
                  pl.BlockSpec((None, SUBLANES, pw), lambda bi, i: (bi, jnp.minimum((i + 1) * per, nblk - 1), 0)),
                  row(pw), row(2 * w), full(wup), row(2 * w), full(aup), full(gup),
                  row(w), row(w), row(w), full(seg)],
        out_specs=[out_spec] * 11,
        out_shape=[jax.ShapeDtypeStruct((b, t, w), F32)] * 11,
        compiler_params=_params("parallel", "parallel"),
        name="rwkv_prep",
    )(p, p, p, mu, w0, wup, a0, aup, gup, k_k, k_a, r_k, seg)


def _wkv_cumsum(lw, reverse):
    c = lw.shape[0]
    ti = lax.broadcasted_iota(jnp.int32, (c, c), 0)
    si = lax.broadcasted_iota(jnp.int32, (c, c), 1)
    cmat = jnp.where((si >= ti) if reverse else (si <= ti), 1.0, 0.0).astype(BF16)
    hi = lw.astype(BF16)
    rem = lw - hi.astype(F32)
    mid = rem.astype(BF16)
    lo = (rem - mid.astype(F32)).astype(BF16)
    return (jnp.dot(cmat, hi, preferred_element_type=F32)
            + jnp.dot(cmat, mid, preferred_element_type=F32)
            + jnp.dot(cmat, lo, preferred_element_type=F32))


def _wkv_chains(chains):
    c = CHUNK
    n2 = 2 * c
    t_id = lax.broadcasted_iota(jnp.int32, (c, LANES), 0)
    lane = lax.broadcasted_iota(jnp.int32, (c, LANES), 1)
    head0 = lane < HEAD
    s_id = lane & (HEAD - 1)
    eye_c = jnp.where(s_id == t_id, 1.0, 0.0)
    ii = lax.broadcasted_iota(jnp.int32, (n2, n2), 0)
    jj = lax.broadcasted_iota(jnp.int32, (n2, n2), 1)
    same = (ii < c) == (jj < c)
    eye = jnp.where(ii == jj, 1.0, 0.0)
    masks = {False: (s_id < t_id, s_id <= t_id), True: (s_id > t_id, s_id >= t_id)}
    each = lambda f, *cols: [f(*xs) for xs in zip(*cols)]
    cat0 = lambda *xs: jnp.concatenate(xs, axis=0)
    cat1 = lambda *xs: jnp.concatenate(xs, axis=1)

    def stack(x):
        return cat0(jnp.where(head0, x, 0.0), jnp.where(head0, 0.0, x))

    r, v, kk, lw, cum, kd, bb, s_refs, rev = [list(col) for col in zip(*chains)]
    strict = [masks[q][0] for q in rev]
    incl = [masks[q][1] for q in rev]

    tot = each(lambda x, q: x[0:1, :] if q else x[c - 1:c, :], cum, rev)
    e_neg = each(lambda x: jnp.exp(-x), cum)
    e_tot = each(lambda tt, x: jnp.exp(tt - x), tot, cum)
    kkt = each(lambda a, x, l: a * jnp.exp(x - l), kk, cum, lw)
    rt = each(lambda a, x: a * jnp.exp(x), r, cum)
    bh = each(jnp.multiply, bb, e_neg)
    kh = each(jnp.multiply, kd, e_neg)
    bp = each(jnp.multiply, bb, e_tot)
    kp = each(jnp.multiply, kd, e_tot)

    g = each(lambda a, b_, c_, d_: _dot_nt(cat0(a, b_), cat0(stack(c_), stack(d_))), kkt, rt, bh, kh)
    a_c = each(lambda m, x: jnp.where(m, x[0:c, 0:LANES], 0.0), strict, g)
    bm_c = each(lambda m, x: jnp.where(m, x[0:c, LANES:2 * LANES], 0.0), strict, g)
    mqb_c = each(lambda m, x: jnp.where(m, x[c:n2, 0:LANES], 0.0), incl, g)
    mqk_c = each(lambda m, x: jnp.where(m, x[c:n2, LANES:2 * LANES], 0.0), incl, g)

    bmv = each(lambda a, b_, v_: _dot(cat0(a, b_), stack(v_)), bm_c, mqk_c, v)

    x = each(lambda a: eye_c - a, a_c)
    ap = each(lambda a: _dot(a, stack(a)), a_c)
    for _ in range((c - 1).bit_length() - 2):
        both = each(lambda p, a: _dot(cat0(p, a), stack(p)), ap, x)
        ap = each(lambda m: m[0:c], both)
        x = each(lambda a, m: a + m[c:n2], x, both)
    x = each(lambda a, p: a + _dot(a, stack(p)), x, ap)

    w12 = each(lambda a, k_, m: _dot(a, cat1(stack(k_), stack(m[0:c]))), x, kkt, bmv)
    qw = each(lambda a, w_: _dot(a, cat1(stack(w_[:, 0:LANES]), stack(w_[:, LANES:2 * LANES]))), mqb_c, w12)
    g1 = each(lambda a, q: a - q[:, 0:LANES], rt, qw)
    g2 = each(lambda m, q: m[c:n2] - q[:, LANES:2 * LANES], bmv, qw)
    m_bd = each(lambda tt, w_, b_: eye * jnp.exp(tt) - jnp.where(same, _dot_tn(w_[:, 0:LANES], b_), 0.0),
                tot, w12, bp)
    n_full = each(lambda v_, w_, k_, b_: _dot_tn(cat0(v_, w_[:, LANES:2 * LANES]), cat0(k_, -b_)),
                  v, w12, kp, bp)
    n_c = each(lambda n: jnp.where(head0, n[0:c], n[c:n2]), n_full)

    s0 = [s_ref[...] for s_ref in s_refs]
    y = each(lambda a, s, b_: _dot_nt(a, stack(s)) + b_, g1, s0, g2)
    s1 = each(lambda s, m, n: _dot(s, m) + n, s0, m_bd, n_c)
    for s_ref, s in zip(s_refs, s1):
        s_ref[...] = s
    return y


def _wkv_kernel(rf, vf, kkf, rb, vb, kkb, lwf, kdf, bbf, lwb, kdb, bbb, yf, yb, s_ref):
    @pl.when(pl.program_id(2) == 0)
    def _():
        s_ref[...] = jnp.zeros_like(s_ref)

    chains = []
    slices = [slice(j * LANES, (j + 1) * LANES) for j in range(rf.shape[1] // LANES)]
    cum_f = _wkv_cumsum(lwf[...], False)
    cum_b = _wkv_cumsum(lwb[...], True)
    for j, sl in enumerate(slices):
        chains.append((rf[:, sl], vf[:, sl], kkf[:, sl], lwf[:, sl], cum_f[:, sl], kdf[:, sl], bbf[:, sl],
                       s_ref.at[0, j], False))
        chains.append((rb[:, sl], vb[:, sl], kkb[:, sl], lwb[:, sl], cum_b[:, sl], kdb[:, sl], bbb[:, sl],
                       s_ref.at[1, j], True))
    ys = _wkv_chains(chains)
    for j, sl in enumerate(slices):
        yf[:, sl] = ys[2 * j]
        yb[:, sl] = ys[2 * j + 1]


def _wkv7_scan(r, v, kk, lw0, kd0, bb0, lw1, kd1, bb1, pairs=WKV_PAIRS):
    b, t, w = r.shape
    nc = t // CHUNK
    lw_blk = pairs * LANES
    blk = (None, CHUNK, lw_blk)
    fwd = pl.BlockSpec(blk, lambda i, p, c: (i, c, p))
    bwd = pl.BlockSpec(blk, lambda i, p, c: (i, nc - 1 - c, p))
    return pl.pallas_call(
        _wkv_kernel,
        grid=(b, w // lw_blk, nc),
        in_specs=[fwd, fwd, fwd, bwd, bwd, bwd, fwd, fwd, fwd, bwd, bwd, bwd],
        out_specs=[fwd, bwd],
        out_shape=[jax.ShapeDtypeStruct((b, t, w), F32)] * 2,
        scratch_shapes=[pltpu.VMEM((2, pairs, HEAD, LANES), F32)],
        compiler_params=_params("parallel", "parallel", "arbitrary"),
        name="wkv7_scan",
    )(r, v, kk, r, v, kk, lw0, kd0, bb0, lw1, kd1, bb1)


def _post_kernel(yf_ref, yb_ref, bonus_ref, g_ref, gg_ref, gb_ref, seg_ref, o_ref):
    seg = seg_ref[...]
    y = yf_ref[...] + yb_ref[...]
    m = _seg_sum(y, seg) * (1.0 / HEAD)
    yc = y - m
    var = _seg_sum(yc * yc, seg) * (1.0 / HEAD)
    yn = yc * lax.rsqrt(var + GN_EPS) * gg_ref[...] + gb_ref[...]
    o_ref[...] = ((yn + bonus_ref[...]) * g_ref[...]).astype(o_ref.dtype)


def _rwkv_post(yf, yb, bonus, g, gn_g, gn_b, seg, tm=512):
    m, w = yf.shape
    tile = pl.BlockSpec((tm, w), lambda i: (i, 0))
    row = pl.BlockSpec((1, w), lambda i: (0, 0))
    return pl.pallas_call(
        _post_kernel,
        grid=(m // tm,),
        in_specs=[tile, tile, tile, tile, row, row, pl.BlockSpec(seg.shape, lambda i: (0, 0))],
        out_specs=tile,
        out_shape=jax.ShapeDtypeStruct((m, w), BF16),
        compiler_params=_params("parallel"),
        name="rwkv_post",
    )(yf, yb, bonus, g, gn_g, gn_b, seg)


def _conv_kernel(gb_ref, gc_ref, h_ref, w_ref, o_ref):
    u = gc_ref[...] * h_ref[...]
    t = u.shape[0]
    rid = lax.broadcasted_iota(jnp.int32, u.shape, 0)
    up = jnp.where(rid == 0, 0.0, pltpu.roll(u, 1, axis=0))
    un = jnp.where(rid == t - 1, 0.0, pltpu.roll(u, t - 1, axis=0))
    hc = w_ref[0:1, :] * up + w_ref[1:2, :] * u + w_ref[2:3, :] * un
    o_ref[...] = (gb_ref[...] * hc).astype(o_ref.dtype)


def _shortconv(p, conv_w, tc=LANES):
    b, t, pw = p.shape
    w = conv_w.shape[1]
    nj = w // tc
    return pl.pallas_call(
        _conv_kernel,
        grid=(b, nj),
        in_specs=[pl.BlockSpec((None, t, tc), lambda bi, j: (bi, 0, j)),
                  pl.BlockSpec((None, t, tc), lambda bi, j: (bi, 0, nj + j)),
                  pl.BlockSpec((None, t, tc), lambda bi, j: (bi, 0, 2 * nj + j)),
                  pl.BlockSpec((conv_w.shape[0], tc), lambda bi, j: (0, j))],
        out_specs=pl.BlockSpec((None, t, tc), lambda bi, j: (bi, 0, j)),
        out_shape=jax.ShapeDtypeStruct((b, t, w), BF16),
        compiler_params=_params("parallel", "parallel"),
        name="shortconv",
    )(p, p, p, conv_w)


def _memkv_kernel(mem_ref, g_ref, b_ref, w_ref, o_ref):
    mn = _layer_norm(mem_ref[...], g_ref[...], b_ref[...])
    o_ref[...] = jnp.dot(mn.astype(BF16), w_ref[...].astype(BF16), preferred_element_type=F32)


def _mem_kv(mem, g, b, w_kv, l, tn=512):
    m, d = mem.shape
    n = w_kv.shape[2]
    return pl.pallas_call(
        _memkv_kernel,
        grid=(n // tn,),
        in_specs=[pl.BlockSpec((m, d), lambda j: (0, 0)),
                  pl.BlockSpec((1, d), lambda j: (0, 0)),
                  pl.BlockSpec((1, d), lambda j: (0, 0)),
                  pl.BlockSpec((None, d, tn), lambda j: (l, 0, j))],
        out_specs=pl.BlockSpec((m, tn), lambda j: (0, j)),
        out_shape=jax.ShapeDtypeStruct((m, n), F32),
        compiler_params=_params("parallel"),
        name="mem_kv",
    )(mem, g, b, w_kv)


def _xattn_kernel(q_ref, k_ref, v_ref, o_ref):
    s = _dot_nt(q_ref[...], k_ref[...]) * (XATTN_HEAD ** -0.5)
    e = jnp.exp(s - jnp.max(s, axis=-1, keepdims=True))
    attn = e / jnp.sum(e, axis=-1, keepdims=True)
    o_ref[...] = _dot(attn, v_ref[...]).astype(o_ref.dtype)


def _xattn(q, kv, tq=512):
    b, t, w = q.shape
    nh = w // XATTN_HEAD
    n_mem = kv.shape[1]
    return pl.pallas_call(
        _xattn_kernel,
        grid=(b, t // tq, nh),
        in_specs=[pl.BlockSpec((None, tq, XATTN_HEAD), lambda bi, i, h: (bi, i, h)),
                  pl.BlockSpec((None, n_mem, XATTN_HEAD), lambda bi, i, h: (bi, 0, h)),
                  pl.BlockSpec((None, n_mem, XATTN_HEAD), lambda bi, i, h: (bi, 0, nh + h))],
        out_specs=pl.BlockSpec((None, tq, XATTN_HEAD), lambda bi, i, h: (bi, i, h)),
        out_shape=jax.ShapeDtypeStruct((b, t, w), BF16),
        compiler_params=_params("parallel", "parallel", "parallel"),
        name="mem_xattn",
    )(q, kv, kv)


def _merge_kernel(alpha, x_ref, xb_ref, y0_ref, y1_ref, y2_ref, wbr_ref, wg0_ref, wg1_ref, wg2_ref,
                  gb_ref, wo_ref, g_ref, b_ref, o_ref, ob_ref):
    j = pl.program_id(1)
    tn = wo_ref.shape[0]

    @pl.when(j == 0)
    def _():
        o_ref[...] = jnp.zeros_like(o_ref)

    xb = xb_ref[...]
    mixed = None
    for n, (y_ref, wg_ref) in enumerate(((y0_ref, wg0_ref), (y1_ref, wg1_ref), (y2_ref, wg2_ref))):
        proj = jnp.dot(y_ref[...], wbr_ref[n], preferred_element_type=F32)
        gate = jax.nn.sigmoid(jnp.dot(xb, wg_ref[...], preferred_element_type=F32) + gb_ref[n])
        mixed = gate * proj if mixed is None else mixed + gate * proj
    mixed = mixed.astype(BF16)
    for n0 in range(0, o_ref.shape[1], tn):
        o_ref[:, n0:n0 + tn] += jnp.dot(mixed, wo_ref[:, n0:n0 + tn], preferred_element_type=F32)

    @pl.when(j == pl.num_programs(1) - 1)
    def _():
        _ln_epilogue(alpha, 1.0, x_ref, g_ref, b_ref, o_ref, ob_ref)


def _merge_block(x, xb, y0, y1, y2, w_branch, w_gate, gate_b, w_out, g, b, alpha, tm=512, tn=512):
    m, d = x.shape
    nb, bw, _ = w_branch.shape
    row_in = lambda n: pl.BlockSpec((tm, n), lambda i, j: (i, 0), pipeline_mode=pl.Buffered(1))
    row_out = pl.BlockSpec((tm, d), lambda i, j: (i, 0))
    gate_cols = lambda n: pl.BlockSpec((d, tn), lambda i, j: (0, n * (d // tn) + j))
    return pl.pallas_call(
        functools.partial(_merge_kernel, alpha),
        grid=(m // tm, d // tn),
        in_specs=[row_in(d), row_in(d), row_in(bw), row_in(bw), row_in(bw),
                  pl.BlockSpec((nb, bw, tn), lambda i, j: (0, 0, j)),
                  gate_cols(0), gate_cols(1), gate_cols(2),
                  pl.BlockSpec((nb, 1, tn), lambda i, j: (0, 0, j)),
                  pl.BlockSpec((tn, d), lambda i, j: (j, 0)),
                  pl.BlockSpec((1, d), lambda i, j: (0, 0)),
                  pl.BlockSpec((1, d), lambda i, j: (0, 0))],
        out_specs=[row_out, row_out],
        out_shape=[jax.ShapeDtypeStruct((m, d), F32), jax.ShapeDtypeStruct((m, d), BF16)],
        compiler_params=_params("parallel", "arbitrary"),
        name="merge_block",
    )(x, xb, y0, y1, y2, w_branch, w_gate, w_gate, w_gate, gate_b, w_out, g, b)


def _block_diag2(w):
    z = jnp.zeros_like(w[0])
    return jnp.concatenate([jnp.concatenate([w[0], z], axis=1),
                            jnp.concatenate([z, w[1]], axis=1)], axis=0)


def kernel(x, mem, ffn1_w_gate, ffn1_w_up, ffn1_w_down, ln1_g, ln1_b, w_in, rwkv_mu, rwkv_w0, rwkv_w_up, rwkv_a0, rwkv_a_up, rwkv_g_up, rwkv_k_k, rwkv_k_a, rwkv_r_k, rwkv_gn_g, rwkv_gn_b, conv_w, mem_ln_g, mem_ln_b, w_mem_kv, w_branch, gate_b, w_out, ln2_g, ln2_b, ffn2_w_gate, ffn2_w_up, ffn2_w_down, ln3_g, ln3_b):
    bsz, t, d = x.shape
    depth = w_in.shape[0]
    n_mem = mem.shape[1]
    rw = rwkv_k_k.shape[1]
    rproj = rwkv_mu.shape[1]
    cw = conv_w.shape[2]
    qw = w_mem_kv.shape[2] // 2
    nb = w_branch.shape[1]
    fdim = ffn1_w_gate.shape[2]
    ffn_tile = 512
    fpad = fdim + (-fdim % ffn_tile)
    alpha = (2 * depth) ** 0.25

    def ffn_weights(wg, wu, wd, l):
        wg = _cast_cols(wg, l, ((0, fdim),), (fpad,), 256)[0]
        wu = _cast_cols(wu, l, ((0, fdim),), (fpad,), 256)[0]
        return wg, wu, _cast_rows(wd, l, fpad, fdim // 8)

    row = lambda a: a.reshape(1, -1)
    head_id = jnp.arange(MXU_DIM) // HEAD
    seg = (head_id[:, None] == head_id[None, :]).astype(BF16)
    xf = x.reshape(bsz * t, d)
    mem2 = mem.reshape(bsz * n_mem, d)
    c0, c1, c2, c3 = rproj, rproj + 3 * cw, rproj + 3 * cw + qw, w_in.shape[2]
    in_cuts = ((0, c0), (c0, c1), (c1, c2), (c2, c3))
    in_widths = tuple(hi - lo for lo, hi in in_cuts)
    w_branch2 = w_branch.reshape(depth, nb * w_branch.shape[2], d)

    for l in range(depth):
        wg, wu, wd = ffn_weights(ffn1_w_gate, ffn1_w_up, ffn1_w_down, l)
        xf, xb = _ffn_block(xf, wg, wu, wd, row(ln1_g[l]), row(ln1_b[l]), alpha, tf=ffn_tile)

        w_r, w_c, w_q, w_gate = _cast_cols(w_in, l, in_cuts, in_widths, 128)
        p_rwkv = _matmul(xb, w_r, 512, c0).reshape(bsz, t, c0)
        p_conv = _matmul(xb, w_c, 512, c1 - c0).reshape(bsz, t, c1 - c0)
        q_mem = _matmul(xb, w_q, 1024, qw).reshape(bsz, t, qw)

        (r, v, kk, lw0, lw1, kd0, kd1, bb0, bb1, g, bonus) = _rwkv_prep(
            p_rwkv, row(rwkv_mu[l]), row(rwkv_w0[l]), _block_diag2(rwkv_w_up[l]),
            row(rwkv_a0[l]), _block_diag2(rwkv_a_up[l]), rwkv_g_up[l],
            row(rwkv_k_k[l]), row(rwkv_k_a[l]), row(rwkv_r_k[l]), seg)
        yf, yb = _wkv7_scan(r, v, kk, lw0, kd0, bb0, lw1, kd1, bb1)
        flat = lambda a: a.reshape(bsz * t, rw)
        y_rwkv = _rwkv_post(flat(yf), flat(yb), flat(bonus), flat(g),
                            row(rwkv_gn_g[l]), row(rwkv_gn_b[l]), seg)

        y_conv = _shortconv(p_conv, conv_w[l]).reshape(bsz * t, cw)

        kv = _mem_kv(mem2, row(mem_ln_g[l]), row(mem_ln_b[l]), w_mem_kv, l)
        y_mem = _xattn(q_mem, kv.reshape(bsz, n_mem, 2 * qw)).reshape(bsz * t, qw)

        w_br = _cast_cols(w_branch2, l, ((0, d),), (d,), 256)[0].reshape(nb, -1, d)
        w_o = _cast_cols(w_out, l, ((0, d),), (d,), 256)[0]
        xf, xb = _merge_block(xf, xb, y_rwkv, y_conv, y_mem, w_br, w_gate,
                              gate_b[l].reshape(nb, 1, d), w_o, row(ln2_g[l]), row(ln2_b[l]), alpha)

        wg, wu, wd = ffn_weights(ffn2_w_gate, ffn2_w_up, ffn2_w_down, l)
        xf, xb = _ffn_block(xf, wg, wu, wd, row(ln3_g[l]), row(ln3_b[l]), alpha, tf=ffn_tile)
    return xf.reshape(bsz, t, d)
```

```python
import functools

import jax
import jax.numpy as jnp
from jax import lax
from jax.experimental import pallas as pl
from jax.experimental.pallas import tpu as pltpu

F32 = jnp.float32
BF16 = jnp.bfloat16

LANES = 128
MXU_DIM = 256
SUBLANES = 8
HEAD = 64
CHUNK = 64
WKV_PAIRS = 8
XATTN_HEAD = 256
DECAY_SCALE = 0.606531
GN_EPS = 64e-5
LN_EPS = 1e-5
KK_EPS = 1e-12
VMEM_LIMIT = 60 * 1024 * 1024


def _params(*semantics):
    return pltpu.CompilerParams(dimension_semantics=semantics, vmem_limit_bytes=VMEM_LIMIT)


def _dot(a, b):
    return jnp.dot(a.astype(BF16), b.astype(BF16), preferred_element_type=F32)


def _dot_nt(a, b):
    return lax.dot_general(a.astype(BF16), b.astype(BF16), (((1,), (1,)), ((), ())),
                           preferred_element_type=F32)


def _dot_tn(a, b):
    return lax.dot_general(a.astype(BF16), b.astype(BF16), (((0,), (0,)), ((), ())),
                           preferred_element_type=F32)


def _split(x):
    hi = x.astype(BF16)
    lo = (x - hi.astype(F32)).astype(BF16)
    return hi, lo


def _dot_split(a, bh, bl):
    ah, al = _split(a)
    return (jnp.dot(ah, bh, preferred_element_type=F32)
            + jnp.dot(ah, bl, preferred_element_type=F32)
            + jnp.dot(al, bh, preferred_element_type=F32))


def _sigmoid(x):
    return 0.5 * jnp.tanh(0.5 * x) + 0.5


def _seg_sum(x, seg):
    hi, lo = _split(x)
    gw = seg.shape[0]
    parts = [jnp.dot(hi[:, j:j + gw], seg, preferred_element_type=F32)
             + jnp.dot(lo[:, j:j + gw], seg, preferred_element_type=F32)
             for j in range(0, x.shape[1], gw)]
    return jnp.concatenate(parts, axis=1)


def _layer_norm(z, g, b):
    mu = jnp.mean(z, axis=-1, keepdims=True)
    zc = z - mu
    var = jnp.mean(zc * zc, axis=-1, keepdims=True)
    return zc * lax.rsqrt(var + LN_EPS) * g + b


def _cast_cols_kernel(cuts, w_ref, *o_refs):
    for (lo, hi), o_ref in zip(cuts, o_refs):
        n = hi - lo
        o_ref[:, 0:n] = w_ref[:, lo:hi].astype(BF16)
        if o_ref.shape[1] > n:
            o_ref[:, n:] = jnp.zeros((o_ref.shape[0], o_ref.shape[1] - n), BF16)


def _cast_cols(w, l, cuts, widths, tr):
    _, rows, cols = w.shape
    return pl.pallas_call(
        functools.partial(_cast_cols_kernel, cuts),
        grid=(rows // tr,),
        in_specs=[pl.BlockSpec((None, tr, cols), lambda i: (l, i, 0))],
        out_specs=[pl.BlockSpec((tr, n), lambda i: (i, 0)) for n in widths],
        out_shape=[jax.ShapeDtypeStruct((rows, n), BF16) for n in widths],
        compiler_params=_params("parallel"),
        name="cast_cols",
    )(w)


def _cast_rows_kernel(n_valid, w_ref, o_ref):
    o_ref[...] = jnp.where(pl.program_id(0) < n_valid, w_ref[...], 0.0).astype(BF16)


def _cast_rows(w, l, rows_out, tr):
    _, rows, cols = w.shape
    n_valid = rows // tr
    return pl.pallas_call(
        functools.partial(_cast_rows_kernel, n_valid),
        grid=(pl.cdiv(rows_out, tr),),
        in_specs=[pl.BlockSpec((None, tr, cols), lambda i: (l, jnp.minimum(i, n_valid - 1), 0))],
        out_specs=pl.BlockSpec((tr, cols), lambda i: (i, 0)),
        out_shape=jax.ShapeDtypeStruct((rows_out, cols), BF16),
        compiler_params=_params("parallel"),
        name="cast_rows",
    )(w)


def _mm_kernel(x_ref, w_ref, o_ref):
    o_ref[...] = jnp.dot(x_ref[...], w_ref[...], preferred_element_type=F32).astype(o_ref.dtype)


def _matmul(x, w, tm, tn, out_dtype=F32):
    m, k = x.shape
    n = w.shape[1]
    w_mode = pl.Buffered(1) if n == tn else None
    return pl.pallas_call(
        _mm_kernel,
        grid=(n // tn, m // tm),
        in_specs=[pl.BlockSpec((tm, k), lambda j, i: (i, 0)),
                  pl.BlockSpec((k, tn), lambda j, i: (0, j), pipeline_mode=w_mode)],
        out_specs=pl.BlockSpec((tm, tn), lambda j, i: (i, j)),
        out_shape=jax.ShapeDtypeStruct((m, n), out_dtype),
        compiler_params=_params("parallel", "parallel"),
        name="matmul",
    )(x, w)


def _add_residual_slab(coef, xs_ref, o_ref):
    f = pl.program_id(1)
    slab = xs_ref.shape[0]

    @pl.when(f < o_ref.shape[0] // slab)
    def _():
        rows = pl.ds(pl.multiple_of(f * slab, slab), slab)
        o_ref[rows, :] += coef * xs_ref[...]


def _ln_epilogue(scale, g_ref, b_ref, o_ref, ob_ref, rows=256):
    for r0 in range(0, o_ref.shape[0], rows):
        sl = slice(r0, r0 + rows)
        y = _layer_norm(scale * o_ref[sl, :], g_ref[...], b_ref[...])
        o_ref[sl, :] = y
        ob_ref[sl, :] = y.astype(BF16)


def _ffn_kernel(alpha, xs_ref, xb_ref, wg_ref, wu_ref, wd_ref, g_ref, b_ref, o_ref, ob_ref):
    f = pl.program_id(1)
    tf = wd_ref.shape[0]

    @pl.when(f == 0)
    def _():
        o_ref[...] = jnp.zeros_like(o_ref)

    half = o_ref.shape[0] // 2
    for r0 in (0, half):
        xb = xb_ref[r0:r0 + half, :]
        hg = jnp.dot(xb, wg_ref[...], preferred_element_type=F32)
        hu = jnp.dot(xb, wu_ref[...], preferred_element_type=F32)
        h = ((hg * jax.nn.sigmoid(hg)) * hu).astype(BF16)
        for n0 in range(0, o_ref.shape[1], tf):
            o_ref[r0:r0 + half, n0:n0 + tf] += jnp.dot(h, wd_ref[:, n0:n0 + tf],
                                                       preferred_element_type=F32)
    _add_residual_slab(2.0 * alpha, xs_ref, o_ref)

    @pl.when(f == pl.num_programs(1) - 1)
    def _():
        _ln_epilogue(0.5, g_ref, b_ref, o_ref, ob_ref)


def _slab_spec(tm, d, n_slabs):
    return pl.BlockSpec((tm // n_slabs, d), lambda i, f: (i * n_slabs + jnp.minimum(f, n_slabs - 1), 0))


def _ffn_block(x, xb, wg, wu, wd, g, b, alpha, tm=1024, tf=512, n_slabs=8):
    m, d = x.shape
    fdim = wg.shape[1]
    assert n_slabs <= fdim // tf
    return pl.pallas_call(
        functools.partial(_ffn_kernel, alpha),
        grid=(m // tm, fdim // tf),
        in_specs=[_slab_spec(tm, d, n_slabs),
                  pl.BlockSpec((tm, d), lambda i, f: (i, 0)),
                  pl.BlockSpec((d, tf), lambda i, f: (0, f)),
                  pl.BlockSpec((d, tf), lambda i, f: (0, f)),
                  pl.BlockSpec((tf, d), lambda i, f: (f, 0)),
                  pl.BlockSpec((1, d), lambda i, f: (0, 0)),
                  pl.BlockSpec((1, d), lambda i, f: (0, 0))],
        out_specs=[pl.BlockSpec((tm, d), lambda i, f: (i, 0)),
                   pl.BlockSpec((tm, d), lambda i, f: (i, 0))],
        out_shape=[jax.ShapeDtypeStruct((m, d), F32), jax.ShapeDtypeStruct((m, d), BF16)],
        compiler_params=_params("parallel", "arbitrary"),
        name="ffn_block",
    )(x, xb, wg, wu, wd, g, b)


def _prep_kernel(width, p_ref, pp_ref, pn_ref, mu_ref, w0_ref, wup_h, wup_l, a0_ref, aup_h, aup_l,
                 gup_h, gup_l, kk_ref, ka_ref, rk_ref, seg_ref,
                 r_o, v_o, kk_o, lw0_o, lw1_o, kd0_o, kd1_o, bb0_o, bb1_o, g_o, bonus_o):
    i = pl.program_id(1)
    p = p_ref[...]
    tm = p.shape[0]
    prev_row = jnp.where(i > 0, pp_ref[SUBLANES - 1:SUBLANES, :], 0.0)
    next_row = jnp.where(i < pl.num_programs(1) - 1, pn_ref[0:1, :], 0.0)
    nb = pltpu.roll(p, 1, axis=0) + pltpu.roll(p, tm - 1, axis=0)
    rid = lax.broadcasted_iota(jnp.int32, (SUBLANES, p.shape[1]), 0)
    first = jnp.where(rid == 0, prev_row + p[1:2, :], nb[0:SUBLANES])
    last = jnp.where(rid == SUBLANES - 1, p[tm - 2:tm - 1, :] + next_row, nb[tm - SUBLANES:tm])
    nb = jnp.concatenate([first, nb[SUBLANES:tm - SUBLANES], last], axis=0)
    ps = p + mu_ref[...] * (0.5 * nb - p)

    w = width
    r = ps[:, 0:w]
    k = ps[:, w:2 * w]
    v = ps[:, 2 * w:3 * w]
    hw = ps[:, 3 * w:3 * w + LANES]
    ha = ps[:, 3 * w + LANES:3 * w + 2 * LANES]
    hg = ps[:, 3 * w + 2 * LANES:3 * w + 3 * LANES]

    lw = -DECAY_SCALE * _sigmoid(w0_ref[...] + _dot_split(jnp.tanh(hw), wup_h[...], wup_l[...]))
    a = _sigmoid(a0_ref[...] + _dot_split(ha, aup_h[...], aup_l[...]))
    g = _dot_split(_sigmoid(hg), gup_h[...], gup_l[...])
    seg = seg_ref[...]
    kkr = k * kk_ref[...]
    kk = kkr * lax.rsqrt(_seg_sum(kkr * kkr, seg) + KK_EPS)
    a0 = a[:, 0:w]
    a1 = a[:, w:2 * w]
    ka = ka_ref[...]
    kd0 = k * (1.0 + (a0 - 1.0) * ka)
    kd1 = k * (1.0 + (a1 - 1.0) * ka)
    bonus = _seg_sum(r * (0.5 * (kd0 + kd1)) * rk_ref[...], seg) * v

    r_o[...] = r
    v_o[...] = v
    kk_o[...] = kk
    lw0_o[...] = lw[:, 0:w]
    lw1_o[...] = lw[:, w:2 * w]
    kd0_o[...] = kd0
    kd1_o[...] = kd1
    bb0_o[...] = kk * a0
    bb1_o[...] = kk * a1
    g_o[...] = g
    bonus_o[...] = bonus


def _rwkv_prep(p, mu, w0, wup, a0, aup, gup, k_k, k_a, r_k, seg, tm=256):
    b, t, pw = p.shape
    w = k_k.shape[1]
    nblk = t // SUBLANES
    per = tm // SUBLANES
    row = lambda n: pl.BlockSpec((1, n), lambda bi, i: (0, 0))
    full = lambda a: pl.BlockSpec(a.shape, lambda bi, i: (0, 0))
    out_spec = pl.BlockSpec((None, tm, w), lambda bi, i: (bi, i, 0))
    return pl.pallas_call(
        functools.partial(_prep_kernel, w),
        grid=(b, t // tm),
        in_specs=[pl.BlockSpec((None, tm, pw), lambda bi, i: (bi, i, 0)),
                  pl.BlockSpec((None, SUBLANES, pw), lambda bi, i: (bi, jnp.maximum(i * per - 1, 0), 0)),
                  pl.BlockSpec((None, SUBLANES, pw), lambda bi, i: (bi, jnp.minimum((i + 1) * per, nblk - 1), 0)),
                  row(pw), row(2 * w), full(wup[0]), full(wup[1]), row(2 * w), full(aup[0]), full(aup[1]),
                  full(gup[0]), full(gup[1]), row(w), row(w), row(w), full(seg)],
        out_specs=[out_spec] * 11,
        out_shape=[jax.ShapeDtypeStruct((b, t, w), F32)] * 11,
        compiler_params=_params("parallel", "parallel"),
        name="rwkv_prep",
    )(p, p, p, mu, w0, *wup, a0, *aup, *gup, k_k, k_a, r_k, seg)


def _wkv_cumsum(lw, reverse):
    c = lw.shape[0]
    ti = lax.broadcasted_iota(jnp.int32, (c, c), 0)
    si = lax.broadcasted_iota(jnp.int32, (c, c), 1)
    cmat = jnp.where((si >= ti) if reverse else (si <= ti), 1.0, 0.0).astype(BF16)
    hi = lw.astype(BF16)
    rem = lw - hi.astype(F32)
    mid = rem.astype(BF16)
    lo = (rem - mid.astype(F32)).astype(BF16)
    return (jnp.dot(cmat, hi, preferred_element_type=F32)
            + jnp.dot(cmat, mid, preferred_element_type=F32)
            + jnp.dot(cmat, lo, preferred_element_type=F32))


def _wkv_chains(chains):
    c = CHUNK
    n2 = 2 * c
    t_id = lax.broadcasted_iota(jnp.int32, (c, LANES), 0)
    lane = lax.broadcasted_iota(jnp.int32, (c, LANES), 1)
    head0 = lane < HEAD
    s_id = lane & (HEAD - 1)
    eye_c = jnp.where(s_id == t_id, 1.0, 0.0)
    ii = lax.broadcasted_iota(jnp.int32, (n2, n2), 0)
    jj = lax.broadcasted_iota(jnp.int32, (n2, n2), 1)
    same = (ii < c) == (jj < c)
    eye = jnp.where(ii == jj, 1.0, 0.0)
    masks = {False: (s_id < t_id, s_id <= t_id), True: (s_id > t_id, s_id >= t_id)}
    each = lambda f, *cols: [f(*xs) for xs in zip(*cols)]
    cat0 = lambda *xs: jnp.concatenate(xs, axis=0)
    cat1 = lambda *xs: jnp.concatenate(xs, axis=1)

    def stack(x):
        return cat0(jnp.where(head0, x, 0.0), jnp.where(head0, 0.0, x))

    r, v, kk, lw, cum, kd, bb, s_refs, rev = [list(col) for col in zip(*chains)]
    strict = [masks[q][0] for q in rev]
    incl = [masks[q][1] for q in rev]

    tot = each(lambda x, q: x[0:1, :] if q else x[c - 1:c, :], cum, rev)
    e_neg = each(lambda x: jnp.exp(-x), cum)
    e_tot = each(lambda tt, x: jnp.exp(tt - x), tot, cum)
    kkt = each(lambda a, x, l: a * jnp.exp(x - l), kk, cum, lw)
    rt = each(lambda a, x: a * jnp.exp(x), r, cum)
    bh = each(jnp.multiply, bb, e_neg)
    kh = each(jnp.multiply, kd, e_neg)
    bp = each(jnp.multiply, bb, e_tot)
    kp = each(jnp.multiply, kd, e_tot)

    g = each(lambda a, b_, c_, d_: _dot_nt(cat0(a, b_), cat0(stack(c_), stack(d_))), kkt, rt, bh, kh)
    a_c = each(lambda m, x: jnp.where(m, x[0:c, 0:LANES], 0.0), strict, g)
    bm_c = each(lambda m, x: jnp.where(m, x[0:c, LANES:2 * LANES], 0.0), strict, g)
    mqb_c = each(lambda m, x: jnp.where(m, x[c:n2, 0:LANES], 0.0), incl, g)
    mqk_c = each(lambda m, x: jnp.where(m, x[c:n2, LANES:2 * LANES], 0.0), incl, g)

    bmv = each(lambda a, b_, v_: _dot(cat0(a, b_), stack(v_)), bm_c, mqk_c, v)

    x = each(lambda a: eye_c - a, a_c)
    ap = each(lambda a: _dot(a, stack(a)), a_c)
    for _ in range((c - 1).bit_length() - 2):
        both = each(lambda p, a: _dot(cat0(p, a), stack(p)), ap, x)
        ap = each(lambda m: m[0:c], both)
        x = each(lambda a, m: a + m[c:n2], x, both)
    x = each(lambda a, p: a + _dot(a, stack(p)), x, ap)

    w12 = each(lambda a, k_, m: _dot(a, cat1(stack(k_), stack(m[0:c]))), x, kkt, bmv)
    qw = each(lambda a, w_: _dot(a, cat1(stack(w_[:, 0:LANES]), stack(w_[:, LANES:2 * LANES]))), mqb_c, w12)
    g1 = each(lambda a, q: a - q[:, 0:LANES], rt, qw)
    g2 = each(lambda m, q: m[c:n2] - q[:, LANES:2 * LANES], bmv, qw)
    m_bd = each(lambda tt, w_, b_: eye * jnp.exp(tt) - jnp.where(same, _dot_tn(w_[:, 0:LANES], b_), 0.0),
                tot, w12, bp)
    n_full = each(lambda v_, w_, k_, b_: _dot_tn(cat0(v_, w_[:, LANES:2 * LANES]), cat0(k_, -b_)),
                  v, w12, kp, bp)
    n_c = each(lambda n: jnp.where(head0, n[0:c], n[c:n2]), n_full)

    s0 = [s_ref[...] for s_ref in s_refs]
    y = each(lambda a, s, b_: _dot_nt(a, stack(s)) + b_, g1, s0, g2)
    s1 = each(lambda s, m, n: _dot(s, m) + n, s0, m_bd, n_c)
    for s_ref, s in zip(s_refs, s1):
        s_ref[...] = s
    return y


def _wkv_kernel(rf, vf, kkf, rb, vb, kkb, lwf, kdf, bbf, lwb, kdb, bbb, yf, yb, s_ref):
    @pl.when(pl.program_id(2) == 0)
    def _():
        s_ref[...] = jnp.zeros_like(s_ref)

    chains = []
    slices = [slice(j * LANES, (j + 1) * LANES) for j in range(rf.shape[1] // LANES)]
    cum_f = _wkv_cumsum(lwf[...], False)
    cum_b = _wkv_cumsum(lwb[...], True)
    for j, sl in enumerate(slices):
        chains.append((rf[:, sl], vf[:, sl], kkf[:, sl], lwf[:, sl], cum_f[:, sl], kdf[:, sl], bbf[:, sl],
                       s_ref.at[0, j], False))
        chains.append((rb[:, sl], vb[:, sl], kkb[:, sl], lwb[:, sl], cum_b[:, sl], kdb[:, sl], bbb[:, sl],
                       s_ref.at[1, j], True))
    ys = _wkv_chains(chains)
    for j, sl in enumerate(slices):
        yf[:, sl] = ys[2 * j]
        yb[:, sl] = ys[2 * j + 1]


def _wkv7_scan(r, v, kk, lw0, kd0, bb0, lw1, kd1, bb1, pairs=WKV_PAIRS):
    b, t, w = r.shape
    nc = t // CHUNK
    lw_blk = pairs * LANES
    blk = (None, CHUNK, lw_blk)
    fwd = pl.BlockSpec(blk, lambda i, p, c: (i, c, p))
    bwd = pl.BlockSpec(blk, lambda i, p, c: (i, nc - 1 - c, p))
    return pl.pallas_call(
        _wkv_kernel,
        grid=(b, w // lw_blk, nc),
        in_specs=[fwd, fwd, fwd, bwd, bwd, bwd, fwd, fwd, fwd, bwd, bwd, bwd],
        out_specs=[fwd, bwd],
        out_shape=[jax.ShapeDtypeStruct((b, t, w), F32)] * 2,
        scratch_shapes=[pltpu.VMEM((2, pairs, HEAD, LANES), F32)],
        compiler_params=_params("parallel", "parallel", "arbitrary"),
        name="wkv7_scan",
    )(r, v, kk, r, v, kk, lw0, kd0, bb0, lw1, kd1, bb1)


def _post_kernel(yf_ref, yb_ref, bonus_ref, g_ref, gg_ref, gb_ref, seg_ref, o_ref):
    seg = seg_ref[...]
    y = yf_ref[...] + yb_ref[...]
    m = _seg_sum(y, seg) * (1.0 / HEAD)
    yc = y - m
    var = _seg_sum(yc * yc, seg) * (1.0 / HEAD)
    yn = yc * lax.rsqrt(var + GN_EPS) * gg_ref[...] + gb_ref[...]
    o_ref[...] = ((yn + bonus_ref[...]) * g_ref[...]).astype(o_ref.dtype)


def _rwkv_post(yf, yb, bonus, g, gn_g, gn_b, seg, tm=512):
    m, w = yf.shape
    tile = pl.BlockSpec((tm, w), lambda i: (i, 0))
    row = pl.BlockSpec((1, w), lambda i: (0, 0))
    return pl.pallas_call(
        _post_kernel,
        grid=(m // tm,),
        in_specs=[tile, tile, tile, tile, row, row, pl.BlockSpec(seg.shape, lambda i: (0, 0))],
        out_specs=tile,
        out_shape=jax.ShapeDtypeStruct((m, w), BF16),
        compiler_params=_params("parallel"),
        name="rwkv_post",
    )(yf, yb, bonus, g, gn_g, gn_b, seg)


def _conv_kernel(gb_ref, gc_ref, h_ref, w_ref, o_ref):
    u = gc_ref[...] * h_ref[...]
    t = u.shape[0]
    rid = lax.broadcasted_iota(jnp.int32, u.shape, 0)
    up = jnp.where(rid == 0, 0.0, pltpu.roll(u, 1, axis=0))
    un = jnp.where(rid == t - 1, 0.0, pltpu.roll(u, t - 1, axis=0))
    hc = w_ref[0:1, :] * up + w_ref[1:2, :] * u + w_ref[2:3, :] * un
    o_ref[...] = (gb_ref[...] * hc).astype(o_ref.dtype)


def _shortconv(p, conv_w, tc=LANES):
    b, t, pw = p.shape
    w = conv_w.shape[1]
    nj = w // tc
    return pl.pallas_call(
        _conv_kernel,
        grid=(b, nj),
        in_specs=[pl.BlockSpec((None, t, tc), lambda bi, j: (bi, 0, j)),
                  pl.BlockSpec((None, t, tc), lambda bi, j: (bi, 0, nj + j)),
                  pl.BlockSpec((None, t, tc), lambda bi, j: (bi, 0, 2 * nj + j)),
                  pl.BlockSpec((conv_w.shape[0], tc), lambda bi, j: (0, j))],
        out_specs=pl.BlockSpec((None, t, tc), lambda bi, j: (bi, 0, j)),
        out_shape=jax.ShapeDtypeStruct((b, t, w), BF16),
        compiler_params=_params("parallel", "parallel"),
        name="shortconv",
    )(p, p, p, conv_w)


def _memkv_kernel(mem_ref, g_ref, b_ref, w_ref, o_ref):
    mn = _layer_norm(mem_ref[...], g_ref[...], b_ref[...])
    o_ref[...] = jnp.dot(mn.astype(BF16), w_ref[...].astype(BF16), preferred_element_type=F32)


def _mem_kv(mem, g, b, w_kv, l, tn=512):
    m, d = mem.shape
    n = w_kv.shape[2]
    return pl.pallas_call(
        _memkv_kernel,
        grid=(n // tn,),
        in_specs=[pl.BlockSpec((m, d), lambda j: (0, 0)),
                  pl.BlockSpec((1, d), lambda j: (0, 0)),
                  pl.BlockSpec((1, d), lambda j: (0, 0)),
                  pl.BlockSpec((None, d, tn), lambda j: (l, 0, j))],
        out_specs=pl.BlockSpec((m, tn), lambda j: (0, j)),
        out_shape=jax.ShapeDtypeStruct((m, n), F32),
        compiler_params=_params("parallel"),
        name="mem_kv",
    )(mem, g, b, w_kv)


def _xattn_kernel(q_ref, k_ref, v_ref, o_ref):
    s = _dot_nt(q_ref[...], k_ref[...]) * (XATTN_HEAD ** -0.5)
    e = jnp.exp(s - jnp.max(s, axis=-1, keepdims=True))
    attn = e / jnp.sum(e, axis=-1, keepdims=True)
    o_ref[...] = _dot(attn, v_ref[...]).astype(o_ref.dtype)


def _xattn(q, kv, tq=512):
    b, t, w = q.shape
    nh = w // XATTN_HEAD
    n_mem = kv.shape[1]
    return pl.pallas_call(
        _xattn_kernel,
        grid=(b, t // tq, nh),
        in_specs=[pl.BlockSpec((None, tq, XATTN_HEAD), lambda bi, i, h: (bi, i, h)),
                  pl.BlockSpec((None, n_mem, XATTN_HEAD), lambda bi, i, h: (bi, 0, h)),
                  pl.BlockSpec((None, n_mem, XATTN_HEAD), lambda bi, i, h: (bi, 0, nh + h))],
        out_specs=pl.BlockSpec((None, tq, XATTN_HEAD), lambda bi, i, h: (bi, i, h)),
        out_shape=jax.ShapeDtypeStruct((b, t, w), BF16),
        compiler_params=_params("parallel", "parallel", "parallel"),
        name="mem_xattn",
    )(q, kv, kv)


def _merge_kernel(alpha, xs_ref, xb_ref, y0_ref, y1_ref, y2_ref, wbr_ref, wg0_ref, wg1_ref, wg2_ref,
                  gb_ref, wo_ref, g_ref, b_ref, o_ref, ob_ref):
    j = pl.program_id(1)
    tn = wo_ref.shape[0]

    @pl.when(j == 0)
    def _():
        o_ref[...] = jnp.zeros_like(o_ref)

    xb = xb_ref[...]
    mixed = None
    for n, (y_ref, wg_ref) in enumerate(((y0_ref, wg0_ref), (y1_ref, wg1_ref), (y2_ref, wg2_ref))):
        proj = jnp.dot(y_ref[...], wbr_ref[n], preferred_element_type=F32)
        gate = jax.nn.sigmoid(jnp.dot(xb, wg_ref[...], preferred_element_type=F32) + gb_ref[n])
        mixed = gate * proj if mixed is None else mixed + gate * proj
    mixed = mixed.astype(BF16)
    for n0 in range(0, o_ref.shape[1], tn):
        o_ref[:, n0:n0 + tn] += jnp.dot(mixed, wo_ref[:, n0:n0 + tn], preferred_element_type=F32)
    _add_residual_slab(alpha, xs_ref, o_ref)

    @pl.when(j == pl.num_programs(1) - 1)
    def _():
        _ln_epilogue(1.0, g_ref, b_ref, o_ref, ob_ref)


def _merge_block(x, xb, y0, y1, y2, w_branch, w_gate, gate_b, w_out, g, b, alpha, tm=512, tn=512):
    m, d = x.shape
    nb, bw, _ = w_branch.shape
    row_in = lambda n: pl.BlockSpec((tm, n), lambda i, j: (i, 0))
    row_out = pl.BlockSpec((tm, d), lambda i, j: (i, 0))
    gate_cols = lambda n: pl.BlockSpec((d, tn), lambda i, j: (0, n * (d // tn) + j))
    return pl.pallas_call(
        functools.partial(_merge_kernel, alpha),
        grid=(m // tm, d // tn),
        in_specs=[_slab_spec(tm, d, d // tn), row_in(d), row_in(bw), row_in(bw), row_in(bw),
                  pl.BlockSpec((nb, bw, tn), lambda i, j: (0, 0, j)),
                  gate_cols(0), gate_cols(1), gate_cols(2),
                  pl.BlockSpec((nb, 1, tn), lambda i, j: (0, 0, j)),
                  pl.BlockSpec((tn, d), lambda i, j: (j, 0)),
                  pl.BlockSpec((1, d), lambda i, j: (0, 0)),
                  pl.BlockSpec((1, d), lambda i, j: (0, 0))],
        out_specs=[row_out, row_out],
        out_shape=[jax.ShapeDtypeStruct((m, d), F32), jax.ShapeDtypeStruct((m, d), BF16)],
        compiler_params=_params("parallel", "arbitrary"),
        name="merge_block",
    )(x, xb, y0, y1, y2, w_branch, w_gate, w_gate, w_gate, gate_b, w_out, g, b)


def _block_diag2(w):
    z = jnp.zeros_like(w[0])
    return jnp.concatenate([jnp.concatenate([w[0], z], axis=1),
                            jnp.concatenate([z, w[1]], axis=1)], axis=0)


def kernel(x, mem, ffn1_w_gate, ffn1_w_up, ffn1_w_down, ln1_g, ln1_b, w_in, rwkv_mu, rwkv_w0, rwkv_w_up, rwkv_a0, rwkv_a_up, rwkv_g_up, rwkv_k_k, rwkv_k_a, rwkv_r_k, rwkv_gn_g, rwkv_gn_b, conv_w, mem_ln_g, mem_ln_b, w_mem_kv, w_branch, gate_b, w_out, ln2_g, ln2_b, ffn2_w_gate, ffn2_w_up, ffn2_w_down, ln3_g, ln3_b):
    bsz, t, d = x.shape
    depth = w_in.shape[0]
    n_mem = mem.shape[1]
    rw = rwkv_k_k.shape[1]
    rproj = rwkv_mu.shape[1]
    cw = conv_w.shape[2]
    qw = w_mem_kv.shape[2] // 2
    nb = w_branch.shape[1]
    fdim = ffn1_w_gate.shape[2]
    ffn_tile = 512
    fpad = fdim + (-fdim % ffn_tile)
    alpha = (2 * depth) ** 0.25

    def ffn_weights(wg, wu, wd, l):
        wg = _cast_cols(wg, l, ((0, fdim),), (fpad,), 256)[0]
        wu = _cast_cols(wu, l, ((0, fdim),), (fpad,), 256)[0]
        return wg, wu, _cast_rows(wd, l, fpad, fdim // 8)

    row = lambda a: a.reshape(1, -1)
    head_id = jnp.arange(MXU_DIM) // HEAD
    seg = (head_id[:, None] == head_id[None, :]).astype(BF16)
    xf = x.reshape(bsz * t, d)
    xb = _cast_cols(xf[None], 0, ((0, d),), (d,), 512)[0]
    mem2 = mem.reshape(bsz * n_mem, d)
    split = lambda a: (a.astype(BF16), (a - a.astype(BF16).astype(F32)).astype(BF16))
    c0, c1, c2, c3 = rproj, rproj + 3 * cw, rproj + 3 * cw + qw, w_in.shape[2]
    in_cuts = ((0, c0), (c0, c1), (c1, c2), (c2, c3))
    in_widths = tuple(hi - lo for lo, hi in in_cuts)
    w_branch2 = w_branch.reshape(depth, nb * w_branch.shape[2], d)

    for l in range(depth):
        wg, wu, wd = ffn_weights(ffn1_w_gate, ffn1_w_up, ffn1_w_down, l)
        xf, xb = _ffn_block(xf, xb, wg, wu, wd, row(ln1_g[l]), row(ln1_b[l]), alpha, tf=ffn_tile)

        w_r, w_c, w_q, w_gate = _cast_cols(w_in, l, in_cuts, in_widths, 128)
        p_rwkv = _matmul(xb, w_r, 512, c0).reshape(bsz, t, c0)
        p_conv = _matmul(xb, w_c, 512, c1 - c0).reshape(bsz, t, c1 - c0)
        q_mem = _matmul(xb, w_q, 1024, qw).reshape(bsz, t, qw)

        (r, v, kk, lw0, lw1, kd0, kd1, bb0, bb1, g, bonus) = _rwkv_prep(
            p_rwkv, row(rwkv_mu[l]), row(rwkv_w0[l]), split(_block_diag2(rwkv_w_up[l])),
            row(rwkv_a0[l]), split(_block_diag2(rwkv_a_up[l])), split(rwkv_g_up[l]),
            row(rwkv_k_k[l]), row(rwkv_k_a[l]), row(rwkv_r_k[l]), seg)
        yf, yb = _wkv7_scan(r, v, kk, lw0, kd0, bb0, lw1, kd1, bb1)
        flat = lambda a: a.reshape(bsz * t, rw)
        y_rwkv = _rwkv_post(flat(yf), flat(yb), flat(bonus), flat(g),
                            row(rwkv_gn_g[l]), row(rwkv_gn_b[l]), seg)

        y_conv = _shortconv(p_conv, conv_w[l]).reshape(bsz * t, cw)

        kv = _mem_kv(mem2, row(mem_ln_g[l]), row(mem_ln_b[l]), w_mem_kv, l)
        y_mem = _xattn(q_mem, kv.reshape(bsz, n_mem, 2 * qw)).reshape(bsz * t, qw)

        w_br = _cast_cols(w_branch2, l, ((0, d),), (d,), 256)[0].reshape(nb, -1, d)
        w_o = _cast_cols(w_out, l, ((0, d),), (d,), 256)[0]
        xf, xb = _merge_block(xf, xb, y_rwkv, y_conv, y_mem, w_br, w_gate,
                              gate_b[l].reshape(nb, 1, d), w_o, row(ln2_g[l]), row(ln2_b[l]), alpha)

        wg, wu, wd = ffn_weights(ffn2_w_gate, ffn2_w_up, ffn2_w_down, l)
        xf, xb = _ffn_block(xf, xb, wg, wu, wd, row(ln3_g[l]), row(ln3_b[l]), alpha, tf=ffn_tile)
    return xf.reshape(bsz, t, d)
```

```python
import functools

import jax
import jax.numpy as jnp
from jax import lax
from jax.experimental import pallas as pl
from jax.experimental.pallas import tpu as pltpu

F32 = jnp.float32
BF16 = jnp.bfloat16

LANES = 128
MXU_DIM = 256
SUBLANES = 8
HEAD = 64
CHUNK = 64
WKV_PAIRS = 8
XATTN_HEAD = 256
DECAY_SCALE = 0.606531
GN_EPS = 64e-5
LN_EPS = 1e-5
KK_EPS = 1e-12
VMEM_LIMIT = 60 * 1024 * 1024


def _params(*semantics):
    return pltpu.CompilerParams(dimension_semantics=semantics, vmem_limit_bytes=VMEM_LIMIT)


def _dot(a, b):
    return jnp.dot(a.astype(BF16), b.astype(BF16), preferred_element_type=F32)


def _dot_nt(a, b):
    return lax.dot_general(a.astype(BF16), b.astype(BF16), (((1,), (1,)), ((), ())),
                           preferred_element_type=F32)


def _dot_tn(a, b):
    return lax.dot_general(a.astype(BF16), b.astype(BF16), (((0,), (0,)), ((), ())),
                           preferred_element_type=F32)


def _split(x):
    hi = x.astype(BF16)
    lo = (x - hi.astype(F32)).astype(BF16)
    return hi, lo


def _dot_split(a, b3):
    ah, al = _split(a)
    return jnp.dot(jnp.concatenate([ah, ah, al], axis=1), b3, preferred_element_type=F32)


def _sigmoid(x):
    return 0.5 * jnp.tanh(0.5 * x) + 0.5


def _seg_sum(x, seg2):
    hi, lo = _split(x)
    gw = seg2.shape[1]
    parts = [jnp.dot(jnp.concatenate([hi[:, j:j + gw], lo[:, j:j + gw]], axis=1), seg2,
                     preferred_element_type=F32)
             for j in range(0, x.shape[1], gw)]
    return jnp.concatenate(parts, axis=1)


def _layer_norm(z, g, b):
    mu = jnp.mean(z, axis=-1, keepdims=True)
    zc = z - mu
    var = jnp.mean(zc * zc, axis=-1, keepdims=True)
    return zc * lax.rsqrt(var + LN_EPS) * g + b


def _cast_cols_kernel(cuts, w_ref, *o_refs):
    for (lo, hi), o_ref in zip(cuts, o_refs):
        n = hi - lo
        o_ref[:, 0:n] = w_ref[:, lo:hi].astype(BF16)
        if o_ref.shape[1] > n:
            o_ref[:, n:] = jnp.zeros((o_ref.shape[0], o_ref.shape[1] - n), BF16)


def _cast_cols(w, l, cuts, widths, tr):
    _, rows, cols = w.shape
    return pl.pallas_call(
        functools.partial(_cast_cols_kernel, cuts),
        grid=(rows // tr,),
        in_specs=[pl.BlockSpec((None, tr, cols), lambda i: (l, i, 0))],
        out_specs=[pl.BlockSpec((tr, n), lambda i: (i, 0)) for n in widths],
        out_shape=[jax.ShapeDtypeStruct((rows, n), BF16) for n in widths],
        compiler_params=_params("parallel"),
        name="cast_cols",
    )(w)


def _cast_rows_kernel(n_valid, w_ref, o_ref):
    o_ref[...] = jnp.where(pl.program_id(0) < n_valid, w_ref[...], 0.0).astype(BF16)


def _cast_rows(w, l, rows_out, tr):
    _, rows, cols = w.shape
    n_valid = rows // tr
    return pl.pallas_call(
        functools.partial(_cast_rows_kernel, n_valid),
        grid=(pl.cdiv(rows_out, tr),),
        in_specs=[pl.BlockSpec((None, tr, cols), lambda i: (l, jnp.minimum(i, n_valid - 1), 0))],
        out_specs=pl.BlockSpec((tr, cols), lambda i: (i, 0)),
        out_shape=jax.ShapeDtypeStruct((rows_out, cols), BF16),
        compiler_params=_params("parallel"),
        name="cast_rows",
    )(w)


def _cast_slab(cw_ref, co_ref):
    n = cw_ref.shape[1]
    co_ref[:, 0:n] = cw_ref[...].astype(BF16)
    if co_ref.shape[1] > n:
        co_ref[:, n:] = jnp.zeros((co_ref.shape[0], co_ref.shape[1] - n), BF16)


def _cast_specs(cast, steps, index):
    cw, l, cols_out = cast
    _, rows, cols = cw.shape
    rs = rows // steps
    assert rs * steps == rows and rs % 16 == 0
    return (pl.BlockSpec((None, rs, cols), lambda *g: (l, index(*g), 0)),
            pl.BlockSpec((rs, cols_out), lambda *g: (index(*g), 0)),
            jax.ShapeDtypeStruct((rows, cols_out), BF16))


def _mm_kernel(x_ref, w_ref, *rest):
    o_ref = rest[-2] if len(rest) == 3 else rest[0]
    o_ref[...] = jnp.dot(x_ref[...], w_ref[...], preferred_element_type=F32).astype(o_ref.dtype)
    if len(rest) == 3:
        _cast_slab(rest[0], rest[2])


def _matmul(x, w, tm, tn, out_dtype=F32, cast=None):
    m, k = x.shape
    n = w.shape[1]
    w_mode = pl.Buffered(1) if n == tn else None
    in_specs = [pl.BlockSpec((tm, k), lambda j, i: (i, 0)),
                pl.BlockSpec((k, tn), lambda j, i: (0, j), pipeline_mode=w_mode)]
    out_specs = [pl.BlockSpec((tm, tn), lambda j, i: (i, j))]
    out_shape = [jax.ShapeDtypeStruct((m, n), out_dtype)]
    args = [x, w]
    if cast is not None:
        assert n == tn
        c_in, c_out, c_shape = _cast_specs(cast, m // tm, lambda j, i: i)
        in_specs.append(c_in)
        out_specs.append(c_out)
        out_shape.append(c_shape)
        args.append(cast[0])
    res = pl.pallas_call(
        _mm_kernel,
        grid=(n // tn, m // tm),
        in_specs=in_specs,
        out_specs=out_specs,
        out_shape=out_shape,
        compiler_params=_params("parallel", "parallel"),
        name="matmul",
    )(*args)
    return res if cast is not None else res[0]


def _add_residual_slab(coef, xs_ref, o_ref):
    f = pl.program_id(1)
    slab = xs_ref.shape[0]

    @pl.when(f < o_ref.shape[0] // slab)
    def _():
        rows = pl.ds(pl.multiple_of(f * slab, slab), slab)
        o_ref[rows, :] += coef * xs_ref[...]


def _ln_epilogue(scale, g_ref, b_ref, o_ref, ob_ref, rows=256):
    for r0 in range(0, o_ref.shape[0], rows):
        sl = slice(r0, r0 + rows)
        y = _layer_norm(scale * o_ref[sl, :], g_ref[...], b_ref[...])
        o_ref[sl, :] = y
        ob_ref[sl, :] = y.astype(BF16)


def _ffn_kernel(alpha, xs_ref, xb_ref, wg_ref, wu_ref, wd_ref, g_ref, b_ref, o_ref, ob_ref):
    f = pl.program_id(1)
    tf = wd_ref.shape[0]

    @pl.when(f == 0)
    def _():
        o_ref[...] = jnp.zeros_like(o_ref)

    half = o_ref.shape[0] // 2
    for r0 in (0, half):
        xb = xb_ref[r0:r0 + half, :]
        hg = jnp.dot(xb, wg_ref[...], preferred_element_type=F32)
        hu = jnp.dot(xb, wu_ref[...], preferred_element_type=F32)
        h = ((hg * jax.nn.sigmoid(hg)) * hu).astype(BF16)
        for n0 in range(0, o_ref.shape[1], tf):
            o_ref[r0:r0 + half, n0:n0 + tf] += jnp.dot(h, wd_ref[:, n0:n0 + tf],
                                                       preferred_element_type=F32)
    _add_residual_slab(2.0 * alpha, xs_ref, o_ref)

    @pl.when(f == pl.num_programs(1) - 1)
    def _():
        _ln_epilogue(0.5, g_ref, b_ref, o_ref, ob_ref)


def _slab_spec(tm, d, n_slabs):
    return pl.BlockSpec((tm // n_slabs, d), lambda i, f: (i * n_slabs + jnp.minimum(f, n_slabs - 1), 0))


def _ffn_block(x, xb, wg, wu, wd, g, b, alpha, tm=1024, tf=512, n_slabs=8):
    m, d = x.shape
    fdim = wg.shape[1]
    assert n_slabs <= fdim // tf
    return pl.pallas_call(
        functools.partial(_ffn_kernel, alpha),
        grid=(m // tm, fdim // tf),
        in_specs=[_slab_spec(tm, d, n_slabs),
                  pl.BlockSpec((tm, d), lambda i, f: (i, 0)),
                  pl.BlockSpec((d, tf), lambda i, f: (0, f)),
                  pl.BlockSpec((d, tf), lambda i, f: (0, f)),
                  pl.BlockSpec((tf, d), lambda i, f: (f, 0)),
                  pl.BlockSpec((1, d), lambda i, f: (0, 0)),
                  pl.BlockSpec((1, d), lambda i, f: (0, 0))],
        out_specs=[pl.BlockSpec((tm, d), lambda i, f: (i, 0)),
                   pl.BlockSpec((tm, d), lambda i, f: (i, 0))],
        out_shape=[jax.ShapeDtypeStruct((m, d), F32), jax.ShapeDtypeStruct((m, d), BF16)],
        compiler_params=_params("parallel", "arbitrary"),
        name="ffn_block",
    )(x, xb, wg, wu, wd, g, b)


def _prep_kernel(width, p_ref, pp_ref, pn_ref, mu_ref, w0_ref, wup_ref, a0_ref, aup_ref, gup_ref,
                 kk_ref, ka_ref, rk_ref, seg_ref,
                 r_o, v_o, kk_o, lw0_o, lw1_o, kd0_o, kd1_o, bb0_o, bb1_o, g_o, bonus_o):
    i = pl.program_id(1)
    p = p_ref[...]
    tm = p.shape[0]
    prev_row = jnp.where(i > 0, pp_ref[SUBLANES - 1:SUBLANES, :], 0.0)
    next_row = jnp.where(i < pl.num_programs(1) - 1, pn_ref[0:1, :], 0.0)
    nb = pltpu.roll(p, 1, axis=0) + pltpu.roll(p, tm - 1, axis=0)
    rid = lax.broadcasted_iota(jnp.int32, (SUBLANES, p.shape[1]), 0)
    first = jnp.where(rid == 0, prev_row + p[1:2, :], nb[0:SUBLANES])
    last = jnp.where(rid == SUBLANES - 1, p[tm - 2:tm - 1, :] + next_row, nb[tm - SUBLANES:tm])
    nb = jnp.concatenate([first, nb[SUBLANES:tm - SUBLANES], last], axis=0)
    ps = p + mu_ref[...] * (0.5 * nb - p)

    w = width
    r = ps[:, 0:w]
    k = ps[:, w:2 * w]
    v = ps[:, 2 * w:3 * w]
    hw = ps[:, 3 * w:3 * w + LANES]
    ha = ps[:, 3 * w + LANES:3 * w + 2 * LANES]
    hg = ps[:, 3 * w + 2 * LANES:3 * w + 3 * LANES]

    lw = -DECAY_SCALE * _sigmoid(w0_ref[...] + _dot_split(jnp.tanh(hw), wup_ref[...]))
    a = _sigmoid(a0_ref[...] + _dot_split(ha, aup_ref[...]))
    g = _dot_split(_sigmoid(hg), gup_ref[...])
    seg = seg_ref[...]
    kkr = k * kk_ref[...]
    kk = kkr * lax.rsqrt(_seg_sum(kkr * kkr, seg) + KK_EPS)
    a0 = a[:, 0:w]
    a1 = a[:, w:2 * w]
    ka = ka_ref[...]
    kd0 = k * (1.0 + (a0 - 1.0) * ka)
    kd1 = k * (1.0 + (a1 - 1.0) * ka)
    bonus = _seg_sum(r * (0.5 * (kd0 + kd1)) * rk_ref[...], seg) * v

    r_o[...] = r
    v_o[...] = v
    kk_o[...] = kk
    lw0_o[...] = lw[:, 0:w]
    lw1_o[...] = lw[:, w:2 * w]
    kd0_o[...] = kd0
    kd1_o[...] = kd1
    bb0_o[...] = kk * a0
    bb1_o[...] = kk * a1
    g_o[...] = g
    bonus_o[...] = bonus


def _rwkv_prep(p, mu, w0, wup, a0, aup, gup, k_k, k_a, r_k, seg, tm=256):
    b, t, pw = p.shape
    w = k_k.shape[1]
    nblk = t // SUBLANES
    per = tm // SUBLANES
    row = lambda n: pl.BlockSpec((1, n), lambda bi, i: (0, 0))
    full = lambda a: pl.BlockSpec(a.shape, lambda bi, i: (0, 0))
    out_spec = pl.BlockSpec((None, tm, w), lambda bi, i: (bi, i, 0))
    return pl.pallas_call(
        functools.partial(_prep_kernel, w),
        grid=(b, t // tm),
        in_specs=[pl.BlockSpec((None, tm, pw), lambda bi, i: (bi, i, 0)),
                  pl.BlockSpec((None, SUBLANES, pw), lambda bi, i: (bi, jnp.maximum(i * per - 1, 0), 0)),
                  pl.BlockSpec((None, SUBLANES, pw), lambda bi, i: (bi, jnp.minimum((i + 1) * per, nblk - 1), 0)),
                  row(pw), row(2 * w), full(wup), row(2 * w), full(aup), full(gup),
                  row(w), row(w), row(w), full(seg)],
        out_specs=[out_spec] * 11,
        out_shape=[jax.ShapeDtypeStruct((b, t, w), F32)] * 11,
        compiler_params=_params("parallel", "parallel"),
        name="rwkv_prep",
    )(p, p, p, mu, w0, wup, a0, aup, gup, k_k, k_a, r_k, seg)


def _wkv_cumsum(lw, reverse):
    c = lw.shape[0]
    ti = lax.broadcasted_iota(jnp.int32, (c, c), 0)
    si = lax.broadcasted_iota(jnp.int32, (c, c), 1)
    cmat = jnp.where((si >= ti) if reverse else (si <= ti), 1.0, 0.0).astype(BF16)
    hi = lw.astype(BF16)
    rem = lw - hi.astype(F32)
    mid = rem.astype(BF16)
    lo = (rem - mid.astype(F32)).astype(BF16)
    return (jnp.dot(cmat, hi, preferred_element_type=F32)
            + jnp.dot(cmat, mid, preferred_element_type=F32)
            + jnp.dot(cmat, lo, preferred_element_type=F32))


def _wkv_chains(chains):
    c = CHUNK
    n2 = 2 * c
    t_id = lax.broadcasted_iota(jnp.int32, (c, LANES), 0)
    lane = lax.broadcasted_iota(jnp.int32, (c, LANES), 1)
    head0 = lane < HEAD
    s_id = lane & (HEAD - 1)
    eye_c = jnp.where(s_id == t_id, 1.0, 0.0)
    ii = lax.broadcasted_iota(jnp.int32, (n2, n2), 0)
    jj = lax.broadcasted_iota(jnp.int32, (n2, n2), 1)
    same = (ii < c) == (jj < c)
    eye = jnp.where(ii == jj, 1.0, 0.0)
    masks = {False: (s_id < t_id, s_id <= t_id), True: (s_id > t_id, s_id >= t_id)}
    each = lambda f, *cols: [f(*xs) for xs in zip(*cols)]
    cat0 = lambda *xs: jnp.concatenate(xs, axis=0)
    cat1 = lambda *xs: jnp.concatenate(xs, axis=1)

    def stack(x):
        return cat0(jnp.where(head0, x, 0.0), jnp.where(head0, 0.0, x))

    r, v, kk, lw, cum, kd, bb, s_refs, rev = [list(col) for col in zip(*chains)]
    strict = [masks[q][0] for q in rev]
    incl = [masks[q][1] for q in rev]

    tot = each(lambda x, q: x[0:1, :] if q else x[c - 1:c, :], cum, rev)
    e_neg = each(lambda x: jnp.exp(-x), cum)
    e_tot = each(lambda tt, x: jnp.exp(tt - x), tot, cum)
    kkt = each(lambda a, x, l: a * jnp.exp(x - l), kk, cum, lw)
    rt = each(lambda a, x: a * jnp.exp(x), r, cum)
    bh = each(jnp.multiply, bb, e_neg)
    kh = each(jnp.multiply, kd, e_neg)
    bp = each(jnp.multiply, bb, e_tot)
    kp = each(jnp.multiply, kd, e_tot)

    g = each(lambda a, b_, c_, d_: _dot_nt(cat0(a, b_), cat0(stack(c_), stack(d_))), kkt, rt, bh, kh)
    a_c = each(lambda m, x: jnp.where(m, x[0:c, 0:LANES], 0.0), strict, g)
    bm_c = each(lambda m, x: jnp.where(m, x[0:c, LANES:2 * LANES], 0.0), strict, g)
    mqb_c = each(lambda m, x: jnp.where(m, x[c:n2, 0:LANES], 0.0), incl, g)
    mqk_c = each(lambda m, x: jnp.where(m, x[c:n2, LANES:2 * LANES], 0.0), incl, g)

    bmv = each(lambda a, b_, v_: _dot(cat0(a, b_), stack(v_)), bm_c, mqk_c, v)

    x = each(lambda a: eye_c - a, a_c)
    ap = each(lambda a: _dot(a, stack(a)), a_c)
    for _ in range((c - 1).bit_length() - 2):
        both = each(lambda p, a: _dot(cat0(p, a), stack(p)), ap, x)
        ap = each(lambda m: m[0:c], both)
        x = each(lambda a, m: a + m[c:n2], x, both)
    x = each(lambda a, p: a + _dot(a, stack(p)), x, ap)

    w12 = each(lambda a, k_, m: _dot(a, cat1(stack(k_), stack(m[0:c]))), x, kkt, bmv)
    qw = each(lambda a, w_: _dot(a, cat1(stack(w_[:, 0:LANES]), stack(w_[:, LANES:2 * LANES]))), mqb_c, w12)
    g1 = each(lambda a, q: a - q[:, 0:LANES], rt, qw)
    g2 = each(lambda m, q: m[c:n2] - q[:, LANES:2 * LANES], bmv, qw)
    m_bd = each(lambda tt, w_, b_: eye * jnp.exp(tt) - jnp.where(same, _dot_tn(w_[:, 0:LANES], b_), 0.0),
                tot, w12, bp)
    n_full = each(lambda v_, w_, k_, b_: _dot_tn(cat0(v_, w_[:, LANES:2 * LANES]), cat0(k_, -b_)),
                  v, w12, kp, bp)
    n_c = each(lambda n: jnp.where(head0, n[0:c], n[c:n2]), n_full)

    s0 = [s_ref[...] for s_ref in s_refs]
    y = each(lambda a, s, b_: _dot_nt(a, stack(s)) + b_, g1, s0, g2)
    s1 = each(lambda s, m, n: _dot(s, m) + n, s0, m_bd, n_c)
    for s_ref, s in zip(s_refs, s1):
        s_ref[...] = s
    return y


def _wkv_kernel(rf, vf, kkf, rb, vb, kkb, lwf, kdf, bbf, lwb, kdb, bbb, yf, yb, s_ref):
    @pl.when(pl.program_id(2) == 0)
    def _():
        s_ref[...] = jnp.zeros_like(s_ref)

    chains = []
    slices = [slice(j * LANES, (j + 1) * LANES) for j in range(rf.shape[1] // LANES)]
    cum_f = _wkv_cumsum(lwf[...], False)
    cum_b = _wkv_cumsum(lwb[...], True)
    for j, sl in enumerate(slices):
        chains.append((rf[:, sl], vf[:, sl], kkf[:, sl], lwf[:, sl], cum_f[:, sl], kdf[:, sl], bbf[:, sl],
                       s_ref.at[0, j], False))
        chains.append((rb[:, sl], vb[:, sl], kkb[:, sl], lwb[:, sl], cum_b[:, sl], kdb[:, sl], bbb[:, sl],
                       s_ref.at[1, j], True))
    ys = _wkv_chains(chains)
    for j, sl in enumerate(slices):
        yf[:, sl] = ys[2 * j]
        yb[:, sl] = ys[2 * j + 1]


def _wkv7_scan(r, v, kk, lw0, kd0, bb0, lw1, kd1, bb1, pairs=WKV_PAIRS):
    b, t, w = r.shape
    nc = t // CHUNK
    lw_blk = pairs * LANES
    blk = (None, CHUNK, lw_blk)
    fwd = pl.BlockSpec(blk, lambda i, p, c: (i, c, p))
    bwd = pl.BlockSpec(blk, lambda i, p, c: (i, nc - 1 - c, p))
    return pl.pallas_call(
        _wkv_kernel,
        grid=(b, w // lw_blk, nc),
        in_specs=[fwd, fwd, fwd, bwd, bwd, bwd, fwd, fwd, fwd, bwd, bwd, bwd],
        out_specs=[fwd, bwd],
        out_shape=[jax.ShapeDtypeStruct((b, t, w), F32)] * 2,
        scratch_shapes=[pltpu.VMEM((2, pairs, HEAD, LANES), F32)],
        compiler_params=_params("parallel", "parallel", "arbitrary"),
        name="wkv7_scan",
    )(r, v, kk, r, v, kk, lw0, kd0, bb0, lw1, kd1, bb1)


def _post_kernel(yf_ref, yb_ref, bonus_ref, g_ref, gg_ref, gb_ref, seg_ref, o_ref):
    seg = seg_ref[...]
    y = yf_ref[...] + yb_ref[...]
    m = _seg_sum(y, seg) * (1.0 / HEAD)
    yc = y - m
    var = _seg_sum(yc * yc, seg) * (1.0 / HEAD)
    yn = yc * lax.rsqrt(var + GN_EPS) * gg_ref[...] + gb_ref[...]
    o_ref[...] = ((yn + bonus_ref[...]) * g_ref[...]).astype(o_ref.dtype)


def _rwkv_post(yf, yb, bonus, g, gn_g, gn_b, seg, tm=512):
    m, w = yf.shape
    tile = pl.BlockSpec((tm, w), lambda i: (i, 0))
    row = pl.BlockSpec((1, w), lambda i: (0, 0))
    return pl.pallas_call(
        _post_kernel,
        grid=(m // tm,),
        in_specs=[tile, tile, tile, tile, row, row, pl.BlockSpec(seg.shape, lambda i: (0, 0))],
        out_specs=tile,
        out_shape=jax.ShapeDtypeStruct((m, w), BF16),
        compiler_params=_params("parallel"),
        name="rwkv_post",
    )(yf, yb, bonus, g, gn_g, gn_b, seg)


def _conv_kernel(gb_ref, gc_ref, h_ref, w_ref, o_ref):
    u = gc_ref[...] * h_ref[...]
    t = u.shape[0]
    rid = lax.broadcasted_iota(jnp.int32, u.shape, 0)
    up = jnp.where(rid == 0, 0.0, pltpu.roll(u, 1, axis=0))
    un = jnp.where(rid == t - 1, 0.0, pltpu.roll(u, t - 1, axis=0))
    hc = w_ref[0:1, :] * up + w_ref[1:2, :] * u + w_ref[2:3, :] * un
    o_ref[...] = (gb_ref[...] * hc).astype(o_ref.dtype)


def _shortconv(p, conv_w, tc=LANES):
    b, t, pw = p.shape
    w = conv_w.shape[1]
    nj = w // tc
    return pl.pallas_call(
        _conv_kernel,
        grid=(b, nj),
        in_specs=[pl.BlockSpec((None, t, tc), lambda bi, j: (bi, 0, j)),
                  pl.BlockSpec((None, t, tc), lambda bi, j: (bi, 0, nj + j)),
                  pl.BlockSpec((None, t, tc), lambda bi, j: (bi, 0, 2 * nj + j)),
                  pl.BlockSpec((conv_w.shape[0], tc), lambda bi, j: (0, j))],
        out_specs=pl.BlockSpec((None, t, tc), lambda bi, j: (bi, 0, j)),
        out_shape=jax.ShapeDtypeStruct((b, t, w), BF16),
        compiler_params=_params("parallel", "parallel"),
        name="shortconv",
    )(p, p, p, conv_w)


def _memkv_kernel(mem_ref, g_ref, b_ref, w_ref, o_ref):
    mn = _layer_norm(mem_ref[...], g_ref[...], b_ref[...])
    o_ref[...] = jnp.dot(mn.astype(BF16), w_ref[...].astype(BF16), preferred_element_type=F32)


def _mem_kv(mem, g, b, w_kv, l, tn=512):
    m, d = mem.shape
    n = w_kv.shape[2]
    return pl.pallas_call(
        _memkv_kernel,
        grid=(n // tn,),
        in_specs=[pl.BlockSpec((m, d), lambda j: (0, 0)),
                  pl.BlockSpec((1, d), lambda j: (0, 0)),
                  pl.BlockSpec((1, d), lambda j: (0, 0)),
                  pl.BlockSpec((None, d, tn), lambda j: (l, 0, j))],
        out_specs=pl.BlockSpec((m, tn), lambda j: (0, j)),
        out_shape=jax.ShapeDtypeStruct((m, n), F32),
        compiler_params=_params("parallel"),
        name="mem_kv",
    )(mem, g, b, w_kv)


def _xattn_kernel(xb_ref, wq_ref, kv_ref, *rest):
    o_ref = rest[-2] if len(rest) == 3 else rest[0]
    q = jnp.dot(xb_ref[...], wq_ref[...], preferred_element_type=F32)
    w = q.shape[1]
    for c in range(0, w, XATTN_HEAD):
        s = _dot_nt(q[:, c:c + XATTN_HEAD], kv_ref[:, c:c + XATTN_HEAD]) * (XATTN_HEAD ** -0.5)
        e = jnp.exp(s - jnp.max(s, axis=-1, keepdims=True))
        attn = e / jnp.sum(e, axis=-1, keepdims=True)
        o_ref[:, c:c + XATTN_HEAD] = _dot(attn, kv_ref[:, w + c:w + c + XATTN_HEAD]).astype(o_ref.dtype)
    if len(rest) == 3:
        _cast_slab(rest[0], rest[2])


def _xattn(xb, w_q, kv, tq=512, cast=None):
    b, t, d = xb.shape
    w = w_q.shape[1]
    n_mem = kv.shape[1]
    nt = t // tq
    in_specs = [pl.BlockSpec((None, tq, d), lambda bi, i: (bi, i, 0)),
                pl.BlockSpec((d, w), lambda bi, i: (0, 0), pipeline_mode=pl.Buffered(1)),
                pl.BlockSpec((None, n_mem, 2 * w), lambda bi, i: (bi, 0, 0))]
    out_specs = [pl.BlockSpec((None, tq, w), lambda bi, i: (bi, i, 0))]
    out_shape = [jax.ShapeDtypeStruct((b, t, w), BF16)]
    args = [xb, w_q, kv]
    if cast is not None:
        c_in, c_out, c_shape = _cast_specs(cast, b * nt, lambda bi, i: bi * nt + i)
        in_specs.append(c_in)
        out_specs.append(c_out)
        out_shape.append(c_shape)
        args.append(cast[0])
    res = pl.pallas_call(
        _xattn_kernel,
        grid=(b, nt),
        in_specs=in_specs,
        out_specs=out_specs,
        out_shape=out_shape,
        compiler_params=_params("parallel", "parallel"),
        name="mem_xattn",
    )(*args)
    return res if cast is not None else res[0]


def _merge_kernel(alpha, xs_ref, xb_ref, y0_ref, y1_ref, y2_ref, wbr_ref, wg0_ref, wg1_ref, wg2_ref,
                  gb_ref, wo_ref, g_ref, b_ref, o_ref, ob_ref):
    j = pl.program_id(1)
    tn = wo_ref.shape[0]

    @pl.when(j == 0)
    def _():
        o_ref[...] = jnp.zeros_like(o_ref)

    xb = xb_ref[...]
    mixed = None
    for n, (y_ref, wg_ref) in enumerate(((y0_ref, wg0_ref), (y1_ref, wg1_ref), (y2_ref, wg2_ref))):
        proj = jnp.dot(y_ref[...], wbr_ref[n], preferred_element_type=F32)
        gate = jax.nn.sigmoid(jnp.dot(xb, wg_ref[...], preferred_element_type=F32) + gb_ref[n])
        mixed = gate * proj if mixed is None else mixed + gate * proj
    mixed = mixed.astype(BF16)
    for n0 in range(0, o_ref.shape[1], tn):
        o_ref[:, n0:n0 + tn] += jnp.dot(mixed, wo_ref[:, n0:n0 + tn], preferred_element_type=F32)
    _add_residual_slab(alpha, xs_ref, o_ref)

    @pl.when(j == pl.num_programs(1) - 1)
    def _():
        _ln_epilogue(1.0, g_ref, b_ref, o_ref, ob_ref)


def _merge_block(x, xb, y0, y1, y2, w_branch, w_gate, gate_b, w_out, g, b, alpha, tm=512, tn=512):
    m, d = x.shape
    nb, bw, _ = w_branch.shape
    row_in = lambda n: pl.BlockSpec((tm, n), lambda i, j: (i, 0))
    row_out = pl.BlockSpec((tm, d), lambda i, j: (i, 0))
    gate_cols = lambda n: pl.BlockSpec((d, tn), lambda i, j: (0, n * (d // tn) + j))
    return pl.pallas_call(
        functools.partial(_merge_kernel, alpha),
        grid=(m // tm, d // tn),
        in_specs=[_slab_spec(tm, d, d // tn), row_in(d), row_in(bw), row_in(bw), row_in(bw),
                  pl.BlockSpec((nb, bw, tn), lambda i, j: (0, 0, j)),
                  gate_cols(0), gate_cols(1), gate_cols(2),
                  pl.BlockSpec((nb, 1, tn), lambda i, j: (0, 0, j)),
                  pl.BlockSpec((tn, d), lambda i, j: (j, 0)),
                  pl.BlockSpec((1, d), lambda i, j: (0, 0)),
                  pl.BlockSpec((1, d), lambda i, j: (0, 0))],
        out_specs=[row_out, row_out],
        out_shape=[jax.ShapeDtypeStruct((m, d), F32), jax.ShapeDtypeStruct((m, d), BF16)],
        compiler_params=_params("parallel", "arbitrary"),
        name="merge_block",
    )(x, xb, y0, y1, y2, w_branch, w_gate, w_gate, w_gate, gate_b, w_out, g, b)


def _block_diag2(w):
    z = jnp.zeros_like(w[0])
    return jnp.concatenate([jnp.concatenate([w[0], z], axis=1),
                            jnp.concatenate([z, w[1]], axis=1)], axis=0)


def kernel(x, mem, ffn1_w_gate, ffn1_w_up, ffn1_w_down, ln1_g, ln1_b, w_in, rwkv_mu, rwkv_w0, rwkv_w_up, rwkv_a0, rwkv_a_up, rwkv_g_up, rwkv_k_k, rwkv_k_a, rwkv_r_k, rwkv_gn_g, rwkv_gn_b, conv_w, mem_ln_g, mem_ln_b, w_mem_kv, w_branch, gate_b, w_out, ln2_g, ln2_b, ffn2_w_gate, ffn2_w_up, ffn2_w_down, ln3_g, ln3_b):
    bsz, t, d = x.shape
    depth = w_in.shape[0]
    n_mem = mem.shape[1]
    rw = rwkv_k_k.shape[1]
    rproj = rwkv_mu.shape[1]
    cw = conv_w.shape[2]
    qw = w_mem_kv.shape[2] // 2
    nb = w_branch.shape[1]
    fdim = ffn1_w_gate.shape[2]
    ffn_tile = 512
    fpad = fdim + (-fdim % ffn_tile)
    alpha = (2 * depth) ** 0.25

    cast_pad = lambda w, l: _cast_cols(w, l, ((0, fdim),), (fpad,), 256)[0]
    cast_down = lambda w, l: _cast_rows(w, l, fpad, fdim // 8)

    row = lambda a: a.reshape(1, -1)
    head_id = jnp.arange(MXU_DIM) // HEAD
    seg = (head_id[:, None] == head_id[None, :]).astype(BF16)
    seg = jnp.concatenate([seg, seg], axis=0)
    xf = x.reshape(bsz * t, d)
    xb = _cast_cols(xf[None], 0, ((0, d),), (d,), 512)[0]
    mem2 = mem.reshape(bsz * n_mem, d)

    def split(a):
        hi = a.astype(BF16)
        return jnp.concatenate([hi, (a - hi.astype(F32)).astype(BF16), hi], axis=0)

    c0, c1, c2, c3 = rproj, rproj + 3 * cw, rproj + 3 * cw + qw, w_in.shape[2]
    in_cuts = ((0, c0), (c0, c1), (c1, c2), (c2, c3))
    in_widths = tuple(hi - lo for lo, hi in in_cuts)
    w_branch2 = w_branch.reshape(depth, nb * w_branch.shape[2], d)

    wg1 = cast_pad(ffn1_w_gate, 0)
    for l in range(depth):
        xf, xb = _ffn_block(xf, xb, wg1, cast_pad(ffn1_w_up, l), cast_down(ffn1_w_down, l),
                            row(ln1_g[l]), row(ln1_b[l]), alpha, tf=ffn_tile)

        w_r, w_c, w_q, w_gate = _cast_cols(w_in, l, in_cuts, in_widths, 128)
        p_rwkv, wg2 = _matmul(xb, w_r, 512, c0, cast=(ffn2_w_gate, l, fpad))
        p_conv, wu2 = _matmul(xb, w_c, 512, c1 - c0, cast=(ffn2_w_up, l, fpad))
        p_rwkv = p_rwkv.reshape(bsz, t, c0)
        p_conv = p_conv.reshape(bsz, t, c1 - c0)

        (r, v, kk, lw0, lw1, kd0, kd1, bb0, bb1, g, bonus) = _rwkv_prep(
            p_rwkv, row(rwkv_mu[l]), row(rwkv_w0[l]), split(_block_diag2(rwkv_w_up[l])),
            row(rwkv_a0[l]), split(_block_diag2(rwkv_a_up[l])), split(rwkv_g_up[l]),
            row(rwkv_k_k[l]), row(rwkv_k_a[l]), row(rwkv_r_k[l]), seg)
        yf, yb = _wkv7_scan(r, v, kk, lw0, kd0, bb0, lw1, kd1, bb1)
        flat = lambda a: a.reshape(bsz * t, rw)
        y_rwkv = _rwkv_post(flat(yf), flat(yb), flat(bonus), flat(g),
                            row(rwkv_gn_g[l]), row(rwkv_gn_b[l]), seg)

        y_conv = _shortconv(p_conv, conv_w[l]).reshape(bsz * t, cw)

        kv = _mem_kv(mem2, row(mem_ln_g[l]), row(mem_ln_b[l]), w_mem_kv, l).reshape(bsz, n_mem, 2 * qw)
        xb3 = xb.reshape(bsz, t, d)
        if l + 1 < depth:
            y_mem, wg1 = _xattn(xb3, w_q, kv, cast=(ffn1_w_gate, l + 1, fpad))
        else:
            y_mem = _xattn(xb3, w_q, kv)
        y_mem = y_mem.reshape(bsz * t, qw)

        w_br = _cast_cols(w_branch2, l, ((0, d),), (d,), 256)[0].reshape(nb, -1, d)
        w_o = _cast_cols(w_out, l, ((0, d),), (d,), 256)[0]
        xf, xb = _merge_block(xf, xb, y_rwkv, y_conv, y_mem, w_br, w_gate,
                              gate_b[l].reshape(nb, 1, d), w_o, row(ln2_g[l]), row(ln2_b[l]), alpha)

        xf, xb = _ffn_block(xf, xb, wg2, wu2, cast_down(ffn2_w_down, l),
                            row(ln3_g[l]), row(ln3_b[l]), alpha, tf=ffn_tile)
    return xf.reshape(bsz, t, d)
```

```python
import functools

import jax
import jax.numpy as jnp
from jax import lax
from jax.experimental import pallas as pl
from jax.experimental.pallas import tpu as pltpu

F32 = jnp.float32
BF16 = jnp.bfloat16

LANES = 128
MXU_DIM = 256
SUBLANES = 8
HEAD = 64
CHUNK = 64
WKV_PAIRS = 8
XATTN_HEAD = 256
DECAY_SCALE = 0.606531
GN_EPS = 64e-5
LN_EPS = 1e-5
KK_EPS = 1e-12
VMEM_LIMIT = 60 * 1024 * 1024


def _params(*semantics):
    return pltpu.CompilerParams(dimension_semantics=semantics, vmem_limit_bytes=VMEM_LIMIT)


def _dot(a, b):
    return jnp.dot(a.astype(BF16), b.astype(BF16), preferred_element_type=F32)


def _dot_nt(a, b):
    return lax.dot_general(a.astype(BF16), b.astype(BF16), (((1,), (1,)), ((), ())),
                           preferred_element_type=F32)


def _dot_tn(a, b):
    return lax.dot_general(a.astype(BF16), b.astype(BF16), (((0,), (0,)), ((), ())),
                           preferred_element_type=F32)


def _split(x):
    hi = x.astype(BF16)
    lo = (x - hi.astype(F32)).astype(BF16)
    return hi, lo


def _dot_split(a, b3):
    ah, al = _split(a)
    return jnp.dot(jnp.concatenate([ah, ah, al], axis=1), b3, preferred_element_type=F32)


def _sigmoid(x):
    return 0.5 * jnp.tanh(0.5 * x) + 0.5


def _seg_sum(x, seg2):
    hi, lo = _split(x)
    gw = seg2.shape[1]
    parts = [jnp.dot(jnp.concatenate([hi[:, j:j + gw], lo[:, j:j + gw]], axis=1), seg2,
                     preferred_element_type=F32)
             for j in range(0, x.shape[1], gw)]
    return jnp.concatenate(parts, axis=1)


def _layer_norm(z, g, b):
    mu = jnp.mean(z, axis=-1, keepdims=True)
    zc = z - mu
    var = jnp.mean(zc * zc, axis=-1, keepdims=True)
    return zc * lax.rsqrt(var + LN_EPS) * g + b


def _cast_cols_kernel(cuts, w_ref, *o_refs):
    for (lo, hi), o_ref in zip(cuts, o_refs):
        n = hi - lo
        o_ref[:, 0:n] = w_ref[:, lo:hi].astype(BF16)
        if o_ref.shape[1] > n:
            o_ref[:, n:] = jnp.zeros((o_ref.shape[0], o_ref.shape[1] - n), BF16)


def _cast_cols(w, l, cuts, widths, tr):
    _, rows, cols = w.shape
    return pl.pallas_call(
        functools.partial(_cast_cols_kernel, cuts),
        grid=(rows // tr,),
        in_specs=[pl.BlockSpec((None, tr, cols), lambda i: (l, i, 0))],
        out_specs=[pl.BlockSpec((tr, n), lambda i: (i, 0)) for n in widths],
        out_shape=[jax.ShapeDtypeStruct((rows, n), BF16) for n in widths],
        compiler_params=_params("parallel"),
        name="cast_cols",
    )(w)


def _cast_rows_kernel(n_valid, w_ref, o_ref):
    o_ref[...] = jnp.where(pl.program_id(0) < n_valid, w_ref[...], 0.0).astype(BF16)


def _cast_rows(w, l, rows_out, tr):
    _, rows, cols = w.shape
    n_valid = rows // tr
    return pl.pallas_call(
        functools.partial(_cast_rows_kernel, n_valid),
        grid=(pl.cdiv(rows_out, tr),),
        in_specs=[pl.BlockSpec((None, tr, cols), lambda i: (l, jnp.minimum(i, n_valid - 1), 0))],
        out_specs=pl.BlockSpec((tr, cols), lambda i: (i, 0)),
        out_shape=jax.ShapeDtypeStruct((rows_out, cols), BF16),
        compiler_params=_params("parallel"),
        name="cast_rows",
    )(w)


def _cast_slab(cw_ref, co_ref):
    n = cw_ref.shape[1]
    co_ref[:, 0:n] = cw_ref[...].astype(BF16)
    if co_ref.shape[1] > n:
        co_ref[:, n:] = jnp.zeros((co_ref.shape[0], co_ref.shape[1] - n), BF16)


def _cast_specs(cast, steps, index):
    cw, l, cols_out = cast
    _, rows, cols = cw.shape
    rs = rows // steps
    assert rs * steps == rows and rs % 16 == 0
    return (pl.BlockSpec((None, rs, cols), lambda *g: (l, index(*g), 0)),
            pl.BlockSpec((rs, cols_out), lambda *g: (index(*g), 0)),
            jax.ShapeDtypeStruct((rows, cols_out), BF16))


def _mm_kernel(x_ref, w_ref, *rest):
    o_ref = rest[-2] if len(rest) == 3 else rest[0]
    o_ref[...] = jnp.dot(x_ref[...], w_ref[...], preferred_element_type=F32).astype(o_ref.dtype)
    if len(rest) == 3:
        _cast_slab(rest[0], rest[2])


def _matmul(x, w, tm, tn, out_dtype=F32, cast=None):
    m, k = x.shape
    n = w.shape[1]
    w_mode = pl.Buffered(1) if n == tn else None
    in_specs = [pl.BlockSpec((tm, k), lambda j, i: (i, 0)),
                pl.BlockSpec((k, tn), lambda j, i: (0, j), pipeline_mode=w_mode)]
    out_specs = [pl.BlockSpec((tm, tn), lambda j, i: (i, j))]
    out_shape = [jax.ShapeDtypeStruct((m, n), out_dtype)]
    args = [x, w]
    if cast is not None:
        assert n == tn
        c_in, c_out, c_shape = _cast_specs(cast, m // tm, lambda j, i: i)
        in_specs.append(c_in)
        out_specs.append(c_out)
        out_shape.append(c_shape)
        args.append(cast[0])
    res = pl.pallas_call(
        _mm_kernel,
        grid=(n // tn, m // tm),
        in_specs=in_specs,
        out_specs=out_specs,
        out_shape=out_shape,
        compiler_params=_params("parallel", "parallel"),
        name="matmul",
    )(*args)
    return res if cast is not None else res[0]


def _add_residual_slab(coef, xs_ref, o_ref):
    f = pl.program_id(1)
    slab = xs_ref.shape[0]

    @pl.when(f < o_ref.shape[0] // slab)
    def _():
        rows = pl.ds(pl.multiple_of(f * slab, slab), slab)
        o_ref[rows, :] += coef * xs_ref[...]


def _ln_epilogue(scale, g_ref, b_ref, o_ref, ob_ref, rows=256):
    for r0 in range(0, o_ref.shape[0], rows):
        sl = slice(r0, r0 + rows)
        y = _layer_norm(scale * o_ref[sl, :], g_ref[...], b_ref[...])
        o_ref[sl, :] = y
        ob_ref[sl, :] = y.astype(BF16)


def _ffn_kernel(alpha, n_valid, xs_ref, xb_ref, wg_ref, wu_ref, wd_ref, g_ref, b_ref, *rest):
    cw_ref, o_ref, ob_ref, co_ref = rest if len(rest) == 4 else (None, *rest, None)
    f = pl.program_id(1)
    tf = wd_ref.shape[0]

    @pl.when(f == 0)
    def _():
        o_ref[...] = jnp.zeros_like(o_ref)

    half = o_ref.shape[0] // 2
    for r0 in (0, half):
        xb = xb_ref[r0:r0 + half, :]
        hg = jnp.dot(xb, wg_ref[...], preferred_element_type=F32)
        hu = jnp.dot(xb, wu_ref[...], preferred_element_type=F32)
        h = ((hg * jax.nn.sigmoid(hg)) * hu).astype(BF16)
        for n0 in range(0, o_ref.shape[1], tf):
            o_ref[r0:r0 + half, n0:n0 + tf] += jnp.dot(h, wd_ref[:, n0:n0 + tf],
                                                       preferred_element_type=F32)
    _add_residual_slab(2.0 * alpha, xs_ref, o_ref)
    if co_ref is not None:
        step = pl.program_id(0) * pl.num_programs(1) + f
        co_ref[...] = jnp.where(step < n_valid, cw_ref[...], 0.0).astype(BF16)

    @pl.when(f == pl.num_programs(1) - 1)
    def _():
        _ln_epilogue(0.5, g_ref, b_ref, o_ref, ob_ref)


def _slab_spec(tm, d, n_slabs):
    return pl.BlockSpec((tm // n_slabs, d), lambda i, f: (i * n_slabs + jnp.minimum(f, n_slabs - 1), 0))


def _ffn_block(x, xb, wg, wu, wd, g, b, alpha, tm=1024, tf=512, n_slabs=8, cast=None):
    m, d = x.shape
    fdim = wg.shape[1]
    nf = fdim // tf
    assert n_slabs <= nf
    in_specs = [_slab_spec(tm, d, n_slabs),
                pl.BlockSpec((tm, d), lambda i, f: (i, 0)),
                pl.BlockSpec((d, tf), lambda i, f: (0, f)),
                pl.BlockSpec((d, tf), lambda i, f: (0, f)),
                pl.BlockSpec((tf, d), lambda i, f: (f, 0)),
                pl.BlockSpec((1, d), lambda i, f: (0, 0)),
                pl.BlockSpec((1, d), lambda i, f: (0, 0))]
    out_specs = [pl.BlockSpec((tm, d), lambda i, f: (i, 0)),
                 pl.BlockSpec((tm, d), lambda i, f: (i, 0))]
    out_shape = [jax.ShapeDtypeStruct((m, d), F32), jax.ShapeDtypeStruct((m, d), BF16)]
    args = [x, xb, wg, wu, wd, g, b]
    n_valid = 0
    if cast is not None:
        cw, l = cast
        steps = (m // tm) * nf
        rs = fdim // steps
        assert rs * steps == fdim and rs % 16 == 0 and cw.shape[1] % rs == 0
        n_valid = cw.shape[1] // rs
        in_specs.append(pl.BlockSpec((None, rs, d), lambda i, f: (l, jnp.minimum(i * nf + f, n_valid - 1), 0)))
        out_specs.append(pl.BlockSpec((rs, d), lambda i, f: (i * nf + f, 0)))
        out_shape.append(jax.ShapeDtypeStruct((fdim, d), BF16))
        args.append(cw)
    return pl.pallas_call(
        functools.partial(_ffn_kernel, alpha, n_valid),
        grid=(m // tm, nf),
        in_specs=in_specs,
        out_specs=out_specs,
        out_shape=out_shape,
        compiler_params=_params("parallel", "arbitrary"),
        name="ffn_block",
    )(*args)


def _prep_kernel(width, p_ref, pp_ref, pn_ref, mu_ref, w0_ref, wup_ref, a0_ref, aup_ref, gup_ref,
                 kk_ref, ka_ref, rk_ref, seg_ref,
                 r_o, v_o, kk_o, lw0_o, lw1_o, kd0_o, kd1_o, bb0_o, bb1_o, g_o, bonus_o):
    i = pl.program_id(1)
    p = p_ref[...]
    tm = p.shape[0]
    prev_row = jnp.where(i > 0, pp_ref[SUBLANES - 1:SUBLANES, :], 0.0)
    next_row = jnp.where(i < pl.num_programs(1) - 1, pn_ref[0:1, :], 0.0)
    nb = pltpu.roll(p, 1, axis=0) + pltpu.roll(p, tm - 1, axis=0)
    rid = lax.broadcasted_iota(jnp.int32, (SUBLANES, p.shape[1]), 0)
    first = jnp.where(rid == 0, prev_row + p[1:2, :], nb[0:SUBLANES])
    last = jnp.where(rid == SUBLANES - 1, p[tm - 2:tm - 1, :] + next_row, nb[tm - SUBLANES:tm])
    nb = jnp.concatenate([first, nb[SUBLANES:tm - SUBLANES], last], axis=0)
    ps = p + mu_ref[...] * (0.5 * nb - p)

    w = width
    r = ps[:, 0:w]
    k = ps[:, w:2 * w]
    v = ps[:, 2 * w:3 * w]
    hw = ps[:, 3 * w:3 * w + LANES]
    ha = ps[:, 3 * w + LANES:3 * w + 2 * LANES]
    hg = ps[:, 3 * w + 2 * LANES:3 * w + 3 * LANES]

    lw = -DECAY_SCALE * _sigmoid(w0_ref[...] + _dot_split(jnp.tanh(hw), wup_ref[...]))
    a = _sigmoid(a0_ref[...] + _dot_split(ha, aup_ref[...]))
    g = _dot_split(_sigmoid(hg), gup_ref[...])
    seg = seg_ref[...]
    kkr = k * kk_ref[...]
    kk = kkr * lax.rsqrt(_seg_sum(kkr * kkr, seg) + KK_EPS)
    a0 = a[:, 0:w]
    a1 = a[:, w:2 * w]
    ka = ka_ref[...]
    kd0 = k * (1.0 + (a0 - 1.0) * ka)
    kd1 = k * (1.0 + (a1 - 1.0) * ka)
    bonus = _seg_sum(r * (0.5 * (kd0 + kd1)) * rk_ref[...], seg) * v

    lw0_o[...] = lw[:, 0:w]
    lw1_o[...] = lw[:, w:2 * w]
    for o_ref, val in ((r_o, r), (v_o, v), (kk_o, kk), (kd0_o, kd0), (kd1_o, kd1),
                       (bb0_o, kk * a0), (bb1_o, kk * a1), (g_o, g), (bonus_o, bonus)):
        o_ref[...] = val.astype(o_ref.dtype)


def _rwkv_prep(p, mu, w0, wup, a0, aup, gup, k_k, k_a, r_k, seg, tm=256):
    b, t, pw = p.shape
    w = k_k.shape[1]
    nblk = t // SUBLANES
    per = tm // SUBLANES
    row = lambda n: pl.BlockSpec((1, n), lambda bi, i: (0, 0))
    full = lambda a: pl.BlockSpec(a.shape, lambda bi, i: (0, 0))
    out_spec = pl.BlockSpec((None, tm, w), lambda bi, i: (bi, i, 0))
    return pl.pallas_call(
        functools.partial(_prep_kernel, w),
        grid=(b, t // tm),
        in_specs=[pl.BlockSpec((None, tm, pw), lambda bi, i: (bi, i, 0)),
                  pl.BlockSpec((None, SUBLANES, pw), lambda bi, i: (bi, jnp.maximum(i * per - 1, 0), 0)),
                  pl.BlockSpec((None, SUBLANES, pw), lambda bi, i: (bi, jnp.minimum((i + 1) * per, nblk - 1), 0)),
                  row(pw), row(2 * w), full(wup), row(2 * w), full(aup), full(gup),
                  row(w), row(w), row(w), full(seg)],
        out_specs=[out_spec] * 11,
        out_shape=[jax.ShapeDtypeStruct((b, t, w), F32 if name in ("lw0", "lw1") else BF16)
                   for name in ("r", "v", "kk", "lw0", "lw1", "kd0", "kd1", "bb0", "bb1", "g", "bonus")],
        compiler_params=_params("parallel", "parallel"),
        name="rwkv_prep",
    )(p, p, p, mu, w0, wup, a0, aup, gup, k_k, k_a, r_k, seg)


def _wkv_cumsum(lw, reverse):
    c = lw.shape[0]
    ti = lax.broadcasted_iota(jnp.int32, (c, c), 0)
    si = lax.broadcasted_iota(jnp.int32, (c, c), 1)
    cmat = jnp.where((si >= ti) if reverse else (si <= ti), 1.0, 0.0).astype(BF16)
    hi = lw.astype(BF16)
    rem = lw - hi.astype(F32)
    mid = rem.astype(BF16)
    lo = (rem - mid.astype(F32)).astype(BF16)
    return (jnp.dot(cmat, hi, preferred_element_type=F32)
            + jnp.dot(cmat, mid, preferred_element_type=F32)
            + jnp.dot(cmat, lo, preferred_element_type=F32))


def _wkv_chains(chains):
    c = CHUNK
    n2 = 2 * c
    t_id = lax.broadcasted_iota(jnp.int32, (c, LANES), 0)
    lane = lax.broadcasted_iota(jnp.int32, (c, LANES), 1)
    head0 = lane < HEAD
    s_id = lane & (HEAD - 1)
    eye_c = jnp.where(s_id == t_id, 1.0, 0.0)
    ii = lax.broadcasted_iota(jnp.int32, (n2, n2), 0)
    jj = lax.broadcasted_iota(jnp.int32, (n2, n2), 1)
    same = (ii < c) == (jj < c)
    eye = jnp.where(ii == jj, 1.0, 0.0)
    masks = {False: (s_id < t_id, s_id <= t_id), True: (s_id > t_id, s_id >= t_id)}
    each = lambda f, *cols: [f(*xs) for xs in zip(*cols)]
    cat0 = lambda *xs: jnp.concatenate(xs, axis=0)
    cat1 = lambda *xs: jnp.concatenate(xs, axis=1)

    def stack(x):
        return cat0(jnp.where(head0, x, 0.0), jnp.where(head0, 0.0, x))

    r, v, kk, lw, cum, kd, bb, s_refs, rev = [list(col) for col in zip(*chains)]
    strict = [masks[q][0] for q in rev]
    incl = [masks[q][1] for q in rev]

    tot = each(lambda x, q: x[0:1, :] if q else x[c - 1:c, :], cum, rev)
    e_neg = each(lambda x: jnp.exp(-x), cum)
    e_tot = each(lambda tt, x: jnp.exp(tt - x), tot, cum)
    kkt = each(lambda a, x, l: a * jnp.exp(x - l), kk, cum, lw)
    rt = each(lambda a, x: a * jnp.exp(x), r, cum)
    bh = each(jnp.multiply, bb, e_neg)
    kh = each(jnp.multiply, kd, e_neg)
    bp = each(jnp.multiply, bb, e_tot)
    kp = each(jnp.multiply, kd, e_tot)

    g = each(lambda a, b_, c_, d_: _dot_nt(cat0(a, b_), cat0(stack(c_), stack(d_))), kkt, rt, bh, kh)
    a_c = each(lambda m, x: jnp.where(m, x[0:c, 0:LANES], 0.0), strict, g)
    bm_c = each(lambda m, x: jnp.where(m, x[0:c, LANES:2 * LANES], 0.0), strict, g)
    mqb_c = each(lambda m, x: jnp.where(m, x[c:n2, 0:LANES], 0.0), incl, g)
    mqk_c = each(lambda m, x: jnp.where(m, x[c:n2, LANES:2 * LANES], 0.0), incl, g)

    bmv = each(lambda a, b_, v_: _dot(cat0(a, b_), stack(v_)), bm_c, mqk_c, v)

    x = each(lambda a: eye_c - a, a_c)
    ap = each(lambda a: _dot(a, stack(a)), a_c)
    for _ in range((c - 1).bit_length() - 2):
        both = each(lambda p, a: _dot(cat0(p, a), stack(p)), ap, x)
        ap = each(lambda m: m[0:c], both)
        x = each(lambda a, m: a + m[c:n2], x, both)
    x = each(lambda a, p: a + _dot(a, stack(p)), x, ap)

    w12 = each(lambda a, k_, m: _dot(a, cat1(stack(k_), stack(m[0:c]))), x, kkt, bmv)
    qw = each(lambda a, w_: _dot(a, cat1(stack(w_[:, 0:LANES]), stack(w_[:, LANES:2 * LANES]))), mqb_c, w12)
    g1 = each(lambda a, q: a - q[:, 0:LANES], rt, qw)
    g2 = each(lambda m, q: m[c:n2] - q[:, LANES:2 * LANES], bmv, qw)
    m_bd = each(lambda tt, w_, b_: eye * jnp.exp(tt) - jnp.where(same, _dot_tn(w_[:, 0:LANES], b_), 0.0),
                tot, w12, bp)
    n_full = each(lambda v_, w_, k_, b_: _dot_tn(cat0(v_, w_[:, LANES:2 * LANES]), cat0(k_, -b_)),
                  v, w12, kp, bp)
    n_c = each(lambda n: jnp.where(head0, n[0:c], n[c:n2]), n_full)

    s0 = [s_ref[...] for s_ref in s_refs]
    y = each(lambda a, s, b_: _dot_nt(a, stack(s)) + b_, g1, s0, g2)
    s1 = each(lambda s, m, n: _dot(s, m) + n, s0, m_bd, n_c)
    for s_ref, s in zip(s_refs, s1):
        s_ref[...] = s
    return y


def _wkv_kernel(rf, vf, kkf, rb, vb, kkb, lwf, kdf, bbf, lwb, kdb, bbb, yf, yb, s_ref):
    @pl.when(pl.program_id(2) == 0)
    def _():
        s_ref[...] = jnp.zeros_like(s_ref)

    chains = []
    slices = [slice(j * LANES, (j + 1) * LANES) for j in range(rf.shape[1] // LANES)]
    cum_f = _wkv_cumsum(lwf[...], False)
    cum_b = _wkv_cumsum(lwb[...], True)
    up = lambda ref, sl: ref[:, sl].astype(F32)
    for j, sl in enumerate(slices):
        chains.append((up(rf, sl), up(vf, sl), up(kkf, sl), lwf[:, sl], cum_f[:, sl], up(kdf, sl), up(bbf, sl),
                       s_ref.at[0, j], False))
        chains.append((up(rb, sl), up(vb, sl), up(kkb, sl), lwb[:, sl], cum_b[:, sl], up(kdb, sl), up(bbb, sl),
                       s_ref.at[1, j], True))
    ys = _wkv_chains(chains)
    for j, sl in enumerate(slices):
        yf[:, sl] = ys[2 * j]
        yb[:, sl] = ys[2 * j + 1]


def _wkv7_scan(r, v, kk, lw0, kd0, bb0, lw1, kd1, bb1, pairs=WKV_PAIRS):
    b, t, w = r.shape
    nc = t // CHUNK
    lw_blk = pairs * LANES
    blk = (None, CHUNK, lw_blk)
    fwd = pl.BlockSpec(blk, lambda i, p, c: (i, c, p))
    bwd = pl.BlockSpec(blk, lambda i, p, c: (i, nc - 1 - c, p))
    return pl.pallas_call(
        _wkv_kernel,
        grid=(b, w // lw_blk, nc),
        in_specs=[fwd, fwd, fwd, bwd, bwd, bwd, fwd, fwd, fwd, bwd, bwd, bwd],
        out_specs=[fwd, bwd],
        out_shape=[jax.ShapeDtypeStruct((b, t, w), F32)] * 2,
        scratch_shapes=[pltpu.VMEM((2, pairs, HEAD, LANES), F32)],
        compiler_params=_params("parallel", "parallel", "arbitrary"),
        name="wkv7_scan",
    )(r, v, kk, r, v, kk, lw0, kd0, bb0, lw1, kd1, bb1)


def _post_kernel(yf_ref, yb_ref, bonus_ref, g_ref, gg_ref, gb_ref, seg_ref, o_ref):
    seg = seg_ref[...]
    y = yf_ref[...] + yb_ref[...]
    m = _seg_sum(y, seg) * (1.0 / HEAD)
    yc = y - m
    var = _seg_sum(yc * yc, seg) * (1.0 / HEAD)
    yn = yc * lax.rsqrt(var + GN_EPS) * gg_ref[...] + gb_ref[...]
    o_ref[...] = ((yn + bonus_ref[...]) * g_ref[...]).astype(o_ref.dtype)


def _rwkv_post(yf, yb, bonus, g, gn_g, gn_b, seg, tm=512):
    m, w = yf.shape
    tile = pl.BlockSpec((tm, w), lambda i: (i, 0))
    row = pl.BlockSpec((1, w), lambda i: (0, 0))
    return pl.pallas_call(
        _post_kernel,
        grid=(m // tm,),
        in_specs=[tile, tile, tile, tile, row, row, pl.BlockSpec(seg.shape, lambda i: (0, 0))],
        out_specs=tile,
        out_shape=jax.ShapeDtypeStruct((m, w), BF16),
        compiler_params=_params("parallel"),
        name="rwkv_post",
    )(yf, yb, bonus, g, gn_g, gn_b, seg)


def _conv_kernel(gb_ref, gc_ref, h_ref, w_ref, o_ref):
    u = gc_ref[...] * h_ref[...]
    t = u.shape[0]
    rid = lax.broadcasted_iota(jnp.int32, u.shape, 0)
    up = jnp.where(rid == 0, 0.0, pltpu.roll(u, 1, axis=0))
    un = jnp.where(rid == t - 1, 0.0, pltpu.roll(u, t - 1, axis=0))
    hc = w_ref[0:1, :] * up + w_ref[1:2, :] * u + w_ref[2:3, :] * un
    o_ref[...] = (gb_ref[...] * hc).astype(o_ref.dtype)


def _shortconv(p, conv_w, tc=LANES):
    b, t, pw = p.shape
    w = conv_w.shape[1]
    nj = w // tc
    return pl.pallas_call(
        _conv_kernel,
        grid=(b, nj),
        in_specs=[pl.BlockSpec((None, t, tc), lambda bi, j: (bi, 0, j)),
                  pl.BlockSpec((None, t, tc), lambda bi, j: (bi, 0, nj + j)),
                  pl.BlockSpec((None, t, tc), lambda bi, j: (bi, 0, 2 * nj + j)),
                  pl.BlockSpec((conv_w.shape[0], tc), lambda bi, j: (0, j))],
        out_specs=pl.BlockSpec((None, t, tc), lambda bi, j: (bi, 0, j)),
        out_shape=jax.ShapeDtypeStruct((b, t, w), BF16),
        compiler_params=_params("parallel", "parallel"),
        name="shortconv",
    )(p, p, p, conv_w)


def _memkv_kernel(mem_ref, g_ref, b_ref, w_ref, o_ref):
    mn = _layer_norm(mem_ref[...], g_ref[...], b_ref[...])
    o_ref[...] = jnp.dot(mn.astype(BF16), w_ref[...].astype(BF16), preferred_element_type=F32)


def _mem_kv(mem, g, b, w_kv, l, tn=512):
    m, d = mem.shape
    n = w_kv.shape[2]
    return pl.pallas_call(
        _memkv_kernel,
        grid=(n // tn,),
        in_specs=[pl.BlockSpec((m, d), lambda j: (0, 0)),
                  pl.BlockSpec((1, d), lambda j: (0, 0)),
                  pl.BlockSpec((1, d), lambda j: (0, 0)),
                  pl.BlockSpec((None, d, tn), lambda j: (l, 0, j))],
        out_specs=pl.BlockSpec((m, tn), lambda j: (0, j)),
        out_shape=jax.ShapeDtypeStruct((m, n), F32),
        compiler_params=_params("parallel"),
        name="mem_kv",
    )(mem, g, b, w_kv)


def _xattn_kernel(xb_ref, wq_ref, kv_ref, *rest):
    o_ref = rest[-2] if len(rest) == 3 else rest[0]
    q = jnp.dot(xb_ref[...], wq_ref[...], preferred_element_type=F32)
    w = q.shape[1]
    for c in range(0, w, XATTN_HEAD):
        s = _dot_nt(q[:, c:c + XATTN_HEAD], kv_ref[:, c:c + XATTN_HEAD]) * (XATTN_HEAD ** -0.5)
        e = jnp.exp(s - jnp.max(s, axis=-1, keepdims=True))
        attn = e / jnp.sum(e, axis=-1, keepdims=True)
        o_ref[:, c:c + XATTN_HEAD] = _dot(attn, kv_ref[:, w + c:w + c + XATTN_HEAD]).astype(o_ref.dtype)
    if len(rest) == 3:
        _cast_slab(rest[0], rest[2])


def _xattn(xb, w_q, kv, tq=512, cast=None):
    b, t, d = xb.shape
    w = w_q.shape[1]
    n_mem = kv.shape[1]
    nt = t // tq
    in_specs = [pl.BlockSpec((None, tq, d), lambda bi, i: (bi, i, 0)),
                pl.BlockSpec((d, w), lambda bi, i: (0, 0), pipeline_mode=pl.Buffered(1)),
                pl.BlockSpec((None, n_mem, 2 * w), lambda bi, i: (bi, 0, 0))]
    out_specs = [pl.BlockSpec((None, tq, w), lambda bi, i: (bi, i, 0))]
    out_shape = [jax.ShapeDtypeStruct((b, t, w), BF16)]
    args = [xb, w_q, kv]
    if cast is not None:
        c_in, c_out, c_shape = _cast_specs(cast, b * nt, lambda bi, i: bi * nt + i)
        in_specs.append(c_in)
        out_specs.append(c_out)
        out_shape.append(c_shape)
        args.append(cast[0])
    res = pl.pallas_call(
        _xattn_kernel,
        grid=(b, nt),
        in_specs=in_specs,
        out_specs=out_specs,
        out_shape=out_shape,
        compiler_params=_params("parallel", "parallel"),
        name="mem_xattn",
    )(*args)
    return res if cast is not None else res[0]


def _merge_kernel(alpha, xs_ref, xb_ref, y0_ref, y1_ref, y2_ref, wbr_ref, wg0_ref, wg1_ref, wg2_ref,
                  gb_ref, wo_ref, g_ref, b_ref, o_ref, ob_ref):
    j = pl.program_id(1)
    tn = wo_ref.shape[0]

    @pl.when(j == 0)
    def _():
        o_ref[...] = jnp.zeros_like(o_ref)

    xb = xb_ref[...]
    mixed = None
    for n, (y_ref, wg_ref) in enumerate(((y0_ref, wg0_ref), (y1_ref, wg1_ref), (y2_ref, wg2_ref))):
        proj = jnp.dot(y_ref[...], wbr_ref[n], preferred_element_type=F32)
        gate = jax.nn.sigmoid(jnp.dot(xb, wg_ref[...], preferred_element_type=F32) + gb_ref[n])
        mixed = gate * proj if mixed is None else mixed + gate * proj
    mixed = mixed.astype(BF16)
    for n0 in range(0, o_ref.shape[1], tn):
        o_ref[:, n0:n0 + tn] += jnp.dot(mixed, wo_ref[:, n0:n0 + tn], preferred_element_type=F32)
    _add_residual_slab(alpha, xs_ref, o_ref)

    @pl.when(j == pl.num_programs(1) - 1)
    def _():
        _ln_epilogue(1.0, g_ref, b_ref, o_ref, ob_ref)


def _merge_block(x, xb, y0, y1, y2, w_branch, w_gate, gate_b, w_out, g, b, alpha, tm=512, tn=512):
    m, d = x.shape
    nb, bw, _ = w_branch.shape
    row_in = lambda n: pl.BlockSpec((tm, n), lambda i, j: (i, 0))
    row_out = pl.BlockSpec((tm, d), lambda i, j: (i, 0))
    gate_cols = lambda n: pl.BlockSpec((d, tn), lambda i, j: (0, n * (d // tn) + j))
    return pl.pallas_call(
        functools.partial(_merge_kernel, alpha),
        grid=(m // tm, d // tn),
        in_specs=[_slab_spec(tm, d, d // tn), row_in(d), row_in(bw), row_in(bw), row_in(bw),
                  pl.BlockSpec((nb, bw, tn), lambda i, j: (0, 0, j)),
                  gate_cols(0), gate_cols(1), gate_cols(2),
                  pl.BlockSpec((nb, 1, tn), lambda i, j: (0, 0, j)),
                  pl.BlockSpec((tn, d), lambda i, j: (j, 0)),
                  pl.BlockSpec((1, d), lambda i, j: (0, 0)),
                  pl.BlockSpec((1, d), lambda i, j: (0, 0))],
        out_specs=[row_out, row_out],
        out_shape=[jax.ShapeDtypeStruct((m, d), F32), jax.ShapeDtypeStruct((m, d), BF16)],
        compiler_params=_params("parallel", "arbitrary"),
        name="merge_block",
    )(x, xb, y0, y1, y2, w_branch, w_gate, w_gate, w_gate, gate_b, w_out, g, b)


def _block_diag2(w):
    z = jnp.zeros_like(w[0])
    return jnp.concatenate([jnp.concatenate([w[0], z], axis=1),
                            jnp.concatenate([z, w[1]], axis=1)], axis=0)


def kernel(x, mem, ffn1_w_gate, ffn1_w_up, ffn1_w_down, ln1_g, ln1_b, w_in, rwkv_mu, rwkv_w0, rwkv_w_up, rwkv_a0, rwkv_a_up, rwkv_g_up, rwkv_k_k, rwkv_k_a, rwkv_r_k, rwkv_gn_g, rwkv_gn_b, conv_w, mem_ln_g, mem_ln_b, w_mem_kv, w_branch, gate_b, w_out, ln2_g, ln2_b, ffn2_w_gate, ffn2_w_up, ffn2_w_down, ln3_g, ln3_b):
    bsz, t, d = x.shape
    depth = w_in.shape[0]
    n_mem = mem.shape[1]
    rw = rwkv_k_k.shape[1]
    rproj = rwkv_mu.shape[1]
    cw = conv_w.shape[2]
    qw = w_mem_kv.shape[2] // 2
    nb = w_branch.shape[1]
    fdim = ffn1_w_gate.shape[2]
    ffn_tile = 512
    fpad = fdim + (-fdim % ffn_tile)
    alpha = (2 * depth) ** 0.25

    cast_pad = lambda w, l: _cast_cols(w, l, ((0, fdim),), (fpad,), 256)[0]
    cast_down = lambda w, l: _cast_rows(w, l, fpad, fdim // 8)

    row = lambda a: a.reshape(1, -1)
    head_id = jnp.arange(MXU_DIM) // HEAD
    seg = (head_id[:, None] == head_id[None, :]).astype(BF16)
    seg = jnp.concatenate([seg, seg], axis=0)
    xf = x.reshape(bsz * t, d)
    xb = _cast_cols(xf[None], 0, ((0, d),), (d,), 512)[0]
    mem2 = mem.reshape(bsz * n_mem, d)

    def split(a):
        hi = a.astype(BF16)
        return jnp.concatenate([hi, (a - hi.astype(F32)).astype(BF16), hi], axis=0)

    c0, c1, c2, c3 = rproj, rproj + 3 * cw, rproj + 3 * cw + qw, w_in.shape[2]
    in_cuts = ((0, c0), (c0, c1), (c1, c2), (c2, c3))
    in_widths = tuple(hi - lo for lo, hi in in_cuts)
    w_branch2 = w_branch.reshape(depth, nb * w_branch.shape[2], d)

    wg1 = cast_pad(ffn1_w_gate, 0)
    wd1 = cast_down(ffn1_w_down, 0)
    for l in range(depth):
        xf, xb, wd2 = _ffn_block(xf, xb, wg1, cast_pad(ffn1_w_up, l), wd1, row(ln1_g[l]), row(ln1_b[l]),
                                 alpha, tf=ffn_tile, cast=(ffn2_w_down, l))

        w_r, w_c, w_q, w_gate = _cast_cols(w_in, l, in_cuts, in_widths, 128)
        p_rwkv, wg2 = _matmul(xb, w_r, 512, c0, cast=(ffn2_w_gate, l, fpad))
        p_conv, wu2 = _matmul(xb, w_c, 512, c1 - c0, cast=(ffn2_w_up, l, fpad))
        p_rwkv = p_rwkv.reshape(bsz, t, c0)
        p_conv = p_conv.reshape(bsz, t, c1 - c0)

        (r, v, kk, lw0, lw1, kd0, kd1, bb0, bb1, g, bonus) = _rwkv_prep(
            p_rwkv, row(rwkv_mu[l]), row(rwkv_w0[l]), split(_block_diag2(rwkv_w_up[l])),
            row(rwkv_a0[l]), split(_block_diag2(rwkv_a_up[l])), split(rwkv_g_up[l]),
            row(rwkv_k_k[l]), row(rwkv_k_a[l]), row(rwkv_r_k[l]), seg)
        yf, yb = _wkv7_scan(r, v, kk, lw0, kd0, bb0, lw1, kd1, bb1)
        flat = lambda a: a.reshape(bsz * t, rw)
        y_rwkv = _rwkv_post(flat(yf), flat(yb), flat(bonus), flat(g),
                            row(rwkv_gn_g[l]), row(rwkv_gn_b[l]), seg)

        y_conv = _shortconv(p_conv, conv_w[l]).reshape(bsz * t, cw)

        kv = _mem_kv(mem2, row(mem_ln_g[l]), row(mem_ln_b[l]), w_mem_kv, l).reshape(bsz, n_mem, 2 * qw)
        xb3 = xb.reshape(bsz, t, d)
        if l + 1 < depth:
            y_mem, wg1 = _xattn(xb3, w_q, kv, cast=(ffn1_w_gate, l + 1, fpad))
        else:
            y_mem = _xattn(xb3, w_q, kv)
        y_mem = y_mem.reshape(bsz * t, qw)

        w_br = _cast_cols(w_branch2, l, ((0, d),), (d,), 256)[0].reshape(nb, -1, d)
        w_o = _cast_cols(w_out, l, ((0, d),), (d,), 256)[0]
        xf, xb = _merge_block(xf, xb, y_rwkv, y_conv, y_mem, w_br, w_gate,
                              gate_b[l].reshape(nb, 1, d), w_o, row(ln2_g[l]), row(ln2_b[l]), alpha)

        if l + 1 < depth:
            xf, xb, wd1 = _ffn_block(xf, xb, wg2, wu2, wd2, row(ln3_g[l]), row(ln3_b[l]), alpha,
                                     tf=ffn_tile, cast=(ffn1_w_down, l + 1))
        else:
            xf, xb = _ffn_block(xf, xb, wg2, wu2, wd2, row(ln3_g[l]), row(ln3_b[l]), alpha, tf=ffn_tile)
    return xf.reshape(bsz, t, d)
```
